```python
import math
import jax, jax.numpy as jnp
from jax import lax
import numpy as np

D_MODEL = 1024
BATCH = 16
SEQ = 256
DEPTH = 2
DEC_BATCH = 4
DEC_SEQ = 4096
PAST_LEN = 512

GRID_W = 64
HEAD_DIM = 64
AXIS_DIM = HEAD_DIM // 2
GQA_HEADS = 8
GQA_KV_HEADS = 2
DIFF_HEADS = 4
FOURIER_GROUPS = 4
FOURIER_GROUP_DIM = 128
FOURIER_WIDTH = FOURIER_GROUPS * FOURIER_GROUP_DIM
GQA_WIDTH = GQA_HEADS * HEAD_DIM
GQA_KV_WIDTH = GQA_KV_HEADS * HEAD_DIM
DIFF_QK_WIDTH = DIFF_HEADS * 2 * HEAD_DIM
DIFF_V_WIDTH = DIFF_HEADS * 2 * HEAD_DIM
N_BRANCHES = 3
IN_WIDTH = (FOURIER_WIDTH + GQA_WIDTH + 2 * GQA_KV_WIDTH + 2 * DIFF_QK_WIDTH
            + DIFF_V_WIDTH + N_BRANCHES * D_MODEL)
N_EXPERTS = 16
N_EXPERT_GROUPS = 4
EXPERTS_PER_GROUP = N_EXPERTS // N_EXPERT_GROUPS
TOP_K = 2
D_EXPERT = 256
ROPE_THETA = 10000.0
Q_BLOCK = 128
EPS = 1e-6

kernel_name = 'hybrid_fourier_gqa_diffattn_moe_prefix_dit_step'

F32 = jnp.float32


def _split_points():
    sizes = (FOURIER_WIDTH, GQA_WIDTH, GQA_KV_WIDTH, GQA_KV_WIDTH,
             DIFF_QK_WIDTH, DIFF_QK_WIDTH, DIFF_V_WIDTH, N_BRANCHES * D_MODEL)
    pts, acc = [], 0
    for s in sizes[:-1]:
        acc += s
        pts.append(acc)
    return pts


def rms_norm(x, g):
    xf = x.astype(F32)
    y = xf * lax.rsqrt(jnp.mean(xf * xf, axis=-1, keepdims=True) + EPS)
    return (y * g.astype(F32)).astype(x.dtype)


def adaln(cond, w_mod_l, b_mod_l):
    m = jax.nn.silu(cond) @ w_mod_l + b_mod_l
    return jnp.split(m, 6, axis=-1)


def modulate(h, shift, scale):
    return h * (1 + scale[:, None, :]) + shift[:, None, :]


def axial_rope(n_tokens):
    rows = n_tokens // GRID_W
    row = jnp.repeat(jnp.arange(rows), GRID_W).astype(F32)
    col = jnp.tile(jnp.arange(GRID_W), rows).astype(F32)
    inv = ROPE_THETA ** (-(jnp.arange(AXIS_DIM // 2, dtype=F32) * 2.0 / AXIS_DIM))
    ang_r = row[:, None] * inv[None, :]
    ang_c = col[:, None] * inv[None, :]
    ang = jnp.concatenate([ang_r, ang_r, ang_c, ang_c], axis=-1)
    return jnp.cos(ang), jnp.sin(ang)


def _rotate_half_axial(x):
    x1, x2, x3, x4 = jnp.split(x, 4, axis=-1)
    return jnp.concatenate([-x2, x1, -x4, x3], axis=-1)


def apply_rope(x, cos, sin):
    shp = (1, x.shape[1]) + (1,) * (x.ndim - 3) + (HEAD_DIM,)
    xf = x.astype(F32)
    return (xf * cos.reshape(shp) + _rotate_half_axial(xf) * sin.reshape(shp)).astype(x.dtype)


def mixer_projections(h, w_in_l, q_norm_l, k_norm_l):
    B, N, _ = h.shape
    z = jnp.einsum('bnd,de->bne', h, w_in_l)
    u_f, q_g, k_g, v_g, q_d, k_d, v_d, gates = jnp.split(z, _split_points(), axis=-1)
    q_g = rms_norm(q_g.reshape(B, N, GQA_HEADS, HEAD_DIM), q_norm_l)
    k_g = rms_norm(k_g.reshape(B, N, GQA_KV_HEADS, HEAD_DIM), k_norm_l)
    v_g = v_g.reshape(B, N, GQA_KV_HEADS, HEAD_DIM)
    q_d = q_d.reshape(B, N, DIFF_HEADS, 2, HEAD_DIM)
    k_d = k_d.reshape(B, N, DIFF_HEADS, 2, HEAD_DIM)
    v_d = v_d.reshape(B, N, DIFF_HEADS, 2 * HEAD_DIM)
    return u_f, q_g, k_g, v_g, q_d, k_d, v_d, gates


def fourier_mix(u):
    B, N, _ = u.shape
    ug = u.astype(F32).reshape(B, N, FOURIER_GROUPS, FOURIER_GROUP_DIM)
    y = jnp.real(jnp.fft.fftn(ug, axes=(1, 3), norm='ortho'))
    return y.reshape(B, N, FOURIER_WIDTH).astype(u.dtype)


def gqa_attention(q, k, v):
    B, Nq, H, d = q.shape
    G = H // GQA_KV_HEADS
    nblk = Nq // Q_BLOCK
    qb = q.reshape(B, nblk, Q_BLOCK, GQA_KV_HEADS, G, d).transpose(1, 0, 2, 3, 4, 5)
    scale = d ** -0.5

    def one_block(qi):
        s = jnp.einsum('bqhgd,bkhd->bhgqk', qi, k, preferred_element_type=F32) * scale
        p = jax.nn.softmax(s, axis=-1)
        return jnp.einsum('bhgqk,bkhd->bqhgd', p.astype(v.dtype), v)

    o = lax.map(one_block, qb)
    return o.transpose(1, 0, 2, 3, 4, 5).reshape(B, Nq, H * d)


def diff_attention(q, k, v, lam_params, lambda_init, subln_g):
    B, Nq, H, _, d = q.shape
    lp = lam_params.astype(F32)
    lam = jnp.exp(jnp.sum(lp[0] * lp[1])) - jnp.exp(jnp.sum(lp[2] * lp[3])) + lambda_init
    nblk = Nq // Q_BLOCK
    qb = q.reshape(B, nblk, Q_BLOCK, H, 2, d).transpose(1, 0, 2, 3, 4, 5)
    scale = d ** -0.5

    def one_block(qi):
        s = jnp.einsum('bqhmd,bkhmd->bhmqk', qi, k, preferred_element_type=F32) * scale
        p = jax.nn.softmax(s, axis=-1)
        a = p[:, :, 0] - lam * p[:, :, 1]
        return jnp.einsum('bhqk,bkhe->bqhe', a.astype(v.dtype), v)

    o = lax.map(one_block, qb)
    o = o.transpose(1, 0, 2, 3, 4).reshape(B, Nq, H, 2 * d)
    o = rms_norm(o, subln_g) * (1.0 - lambda_init)
    return o.reshape(B, Nq, DIFF_V_WIDTH)


def merge_branches(u_f, o_g, o_d, gates, w_f, w_go, w_do, w_out):
    g_f, g_g, g_d = jnp.split(jax.nn.sigmoid(gates), N_BRANCHES, axis=-1)
    merged = (g_f * (fourier_mix(u_f) @ w_f)
              + g_g * (o_g @ w_go)
              + g_d * (o_d @ w_do))
    return merged @ w_out


def moe(h, w_router, router_bias, w_gate_l, w_up_l, w_down_l):
    B, N, D = h.shape
    t = h.reshape(B * N, D)
    scores = jax.nn.sigmoid((t @ w_router).astype(F32))
    sel = scores + router_bias.astype(F32)
    grp = sel.reshape(-1, N_EXPERT_GROUPS, EXPERTS_PER_GROUP)
    grp_score = lax.top_k(grp, TOP_K)[0].sum(-1)
    best = jnp.argmax(grp_score, axis=-1)
    in_best = jax.nn.one_hot(best, N_EXPERT_GROUPS, dtype=F32)[:, :, None] > 0
    masked = jnp.where(in_best, grp, -jnp.inf).reshape(-1, N_EXPERTS)
    _, idx = lax.top_k(masked, TOP_K)
    w = jnp.take_along_axis(scores, idx, axis=-1)
    w = w / jnp.sum(w, axis=-1, keepdims=True)
    combine = jnp.sum(jax.nn.one_hot(idx, N_EXPERTS, dtype=F32) * w[..., None], axis=1)
    gate = jnp.einsum('td,edf->tef', t, w_gate_l)
    up = jnp.einsum('td,edf->tef', t, w_up_l)
    act = jax.nn.silu(gate) * up * combine[..., None].astype(gate.dtype)
    return jnp.einsum('tef,efd->td', act, w_down_l).reshape(B, N, D)


def setup_inputs(seed: int = 0) -> dict:
    key = jax.random.key(seed)
    ks = jax.random.split(key, 32)
    D = D_MODEL

    def nrm(k, shape, scale):
        return jax.random.normal(k, shape, F32) * scale

    def gain(k, shape):
        return 1.0 + 0.02 * jax.random.normal(k, shape, F32)

    return {
        'x_prompt': nrm(ks[0], (BATCH, SEQ, D), 1.0),
        'x_sample': nrm(ks[1], (DEC_BATCH, DEC_SEQ, D), 1.0),
        'cache_gqa_k': nrm(ks[2], (DEC_BATCH, DEPTH, PAST_LEN, GQA_KV_HEADS, HEAD_DIM), 1.0),
        'cache_gqa_v': nrm(ks[3], (DEC_BATCH, DEPTH, PAST_LEN, GQA_KV_HEADS, HEAD_DIM), 1.0),
        'cache_diff_k': nrm(ks[4], (DEC_BATCH, DEPTH, PAST_LEN, DIFF_HEADS, 2, HEAD_DIM), 1.0),
        'cache_diff_v': nrm(ks[5], (DEC_BATCH, DEPTH, PAST_LEN, DIFF_HEADS, 2 * HEAD_DIM), 1.0),
        'c': nrm(ks[6], (DEC_BATCH, D), 1.0),
        'c_ctx': nrm(ks[7], (D,), 1.0),
        'norm1_g': gain(ks[8], (DEPTH, D)),
        'w_mod': nrm(ks[9], (DEPTH, D, 6 * D), 0.5 * D ** -0.5),
        'b_mod': nrm(ks[10], (DEPTH, 6 * D), 0.01),
        'w_in': nrm(ks[11], (DEPTH, D, IN_WIDTH), D ** -0.5),
        'gqa_q_norm': gain(ks[12], (DEPTH, HEAD_DIM)),
        'gqa_k_norm': gain(ks[13], (DEPTH, HEAD_DIM)),
        'diff_lambda': nrm(ks[14], (DEPTH, 4, HEAD_DIM), 0.1),
        'diff_subln_g': gain(ks[15], (DEPTH, 2 * HEAD_DIM)),
        'w_fourier': nrm(ks[16], (DEPTH, FOURIER_WIDTH, D), FOURIER_WIDTH ** -0.5),
        'w_gqa_o': nrm(ks[17], (DEPTH, GQA_WIDTH, D), GQA_WIDTH ** -0.5),
        'w_diff_o': nrm(ks[18], (DEPTH, DIFF_V_WIDTH, D), DIFF_V_WIDTH ** -0.5),
        'w_out': nrm(ks[19], (DEPTH, D, D), D ** -0.5),
        'norm2_g': gain(ks[20], (DEPTH, D)),
        'w_router': nrm(ks[21], (D, N_EXPERTS), D ** -0.5),
        'router_bias': nrm(ks[22], (N_EXPERTS,), 0.01),
        'w_e_gate': nrm(ks[23], (DEPTH, N_EXPERTS, D, D_EXPERT), D ** -0.5),
        'w_e_up': nrm(ks[24], (DEPTH, N_EXPERTS, D, D_EXPERT), D ** -0.5),
        'w_e_down': nrm(ks[25], (DEPTH, N_EXPERTS, D_EXPERT, D), D_EXPERT ** -0.5),
        'final_norm_g': gain(ks[26], (D,)),
    }


def reference(x_prompt, x_sample, cache_gqa_k, cache_gqa_v, cache_diff_k, cache_diff_v,
              c, c_ctx, norm1_g, w_mod, b_mod, w_in, gqa_q_norm, gqa_k_norm,
              diff_lambda, diff_subln_g, w_fourier, w_gqa_o, w_diff_o, w_out, norm2_g,
              w_router, router_bias, w_e_gate, w_e_up, w_e_down, final_norm_g):
    xp = x_prompt
    ks_g, vs_g, ks_d, vs_d = [], [], [], []
    for l in range(DEPTH):
        lambda_init = 0.8 - 0.6 * math.exp(-0.3 * l)
        sh1, sc1, g1, sh2, sc2, g2 = adaln(c_ctx[None, :], w_mod[l], b_mod[l])
        h = modulate(rms_norm(xp, norm1_g[l]), sh1, sc1)
        u_f, q_g, k_g, v_g, q_d, k_d, v_d, gates = mixer_projections(
            h, w_in[l], gqa_q_norm[l], gqa_k_norm[l])
        o_g = gqa_attention(q_g, k_g, v_g)
        o_d = diff_attention(q_d, k_d, v_d, diff_lambda[l], lambda_init, diff_subln_g[l])
        xp = xp + g1[:, None, :] * merge_branches(u_f, o_g, o_d, gates, w_fourier[l],
                                                  w_gqa_o[l], w_diff_o[l], w_out[l])
        h = modulate(rms_norm(xp, norm2_g[l]), sh2, sc2)
        xp = xp + g2[:, None, :] * moe(h, w_router, router_bias, w_e_gate[l], w_e_up[l], w_e_down[l])
        ks_g.append(k_g)
        vs_g.append(v_g)
        ks_d.append(k_d)
        vs_d.append(v_d)
    y_prompt = rms_norm(xp, final_norm_g)
    new_gqa_k = jnp.stack(ks_g, axis=1)
    new_gqa_v = jnp.stack(vs_g, axis=1)
    new_diff_k = jnp.stack(ks_d, axis=1)
    new_diff_v = jnp.stack(vs_d, axis=1)

    cos, sin = axial_rope(x_sample.shape[1])
    xs = x_sample
    for l in range(DEPTH):
        lambda_init = 0.8 - 0.6 * math.exp(-0.3 * l)
        sh1, sc1, g1, sh2, sc2, g2 = adaln(c, w_mod[l], b_mod[l])
        h = modulate(rms_norm(xs, norm1_g[l]), sh1, sc1)
        u_f, q_g, k_g, v_g, q_d, k_d, v_d, gates = mixer_projections(
            h, w_in[l], gqa_q_norm[l], gqa_k_norm[l])
        q_g = apply_rope(q_g, cos, sin)
        k_g = apply_rope(k_g, cos, sin)
        q_d = apply_rope(q_d, cos, sin)
        k_d = apply_rope(k_d, cos, sin)
        k_g = jnp.concatenate([cache_gqa_k[:, l].astype(k_g.dtype), k_g], axis=1)
        v_g = jnp.concatenate([cache_gqa_v[:, l].astype(v_g.dtype), v_g], axis=1)
        k_d = jnp.concatenate([cache_diff_k[:, l].astype(k_d.dtype), k_d], axis=1)
        v_d = jnp.concatenate([cache_diff_v[:, l].astype(v_d.dtype), v_d], axis=1)
        o_g = gqa_attention(q_g, k_g, v_g)
        o_d = diff_attention(q_d, k_d, v_d, diff_lambda[l], lambda_init, diff_subln_g[l])
        xs = xs + g1[:, None, :] * merge_branches(u_f, o_g, o_d, gates, w_fourier[l],
                                                  w_gqa_o[l], w_diff_o[l], w_out[l])
        h = modulate(rms_norm(xs, norm2_g[l]), sh2, sc2)
        xs = xs + g2[:, None, :] * moe(h, w_router, router_bias, w_e_gate[l], w_e_up[l], w_e_down[l])
    y_sample = rms_norm(xs, final_norm_g)

    return (y_prompt, y_sample, new_gqa_k, new_gqa_v, new_diff_k, new_diff_v)
```

```python
import functools
import math

import jax
import jax.numpy as jnp
import numpy as np
from jax import lax
from jax.experimental import pallas as pl
from jax.experimental.pallas import tpu as pltpu

F32 = jnp.float32
BF16 = jnp.bfloat16

D_MODEL = 1024
DEPTH = 2
GRID_W = 64
HEAD_DIM = 64
AXIS_DIM = HEAD_DIM // 2
GQA_HEADS = 8
GQA_KV_HEADS = 2
DIFF_HEADS = 4
FOURIER_GROUPS = 4
FOURIER_GROUP_DIM = 128
FOURIER_WIDTH = FOURIER_GROUPS * FOURIER_GROUP_DIM
GQA_WIDTH = GQA_HEADS * HEAD_DIM
GQA_KV_WIDTH = GQA_KV_HEADS * HEAD_DIM
DIFF_WIDTH = DIFF_HEADS * 2 * HEAD_DIM
N_BRANCHES = 3
GATES_WIDTH = N_BRANCHES * D_MODEL
N_EXPERTS = 16
N_EXPERT_GROUPS = 4
EXPERTS_PER_GROUP = N_EXPERTS // N_EXPERT_GROUPS
D_EXPERT = 256
ROPE_THETA = 10000.0
EPS = 1e-6
QK_SCALE = HEAD_DIM ** -0.5

_OFF_UF = 0
_OFF_QG = _OFF_UF + FOURIER_WIDTH
_OFF_KG = _OFF_QG + GQA_WIDTH
_OFF_VG = _OFF_KG + GQA_KV_WIDTH
_OFF_QD = _OFF_VG + GQA_KV_WIDTH
_OFF_KD = _OFF_QD + DIFF_WIDTH
_OFF_VD = _OFF_KD + DIFF_WIDTH
_OFF_GT = _OFF_VD + DIFF_WIDTH
IN_WIDTH = _OFF_GT + GATES_WIDTH

LANES = 128
VMEM_LIMIT_BYTES = 56 * 1024 * 1024

MOD_ROWS = 8
TOKEN_TILE = 512
ATTN_Q_TILE = 256
ATTN_K_CHUNK = 512
DFT_RADIX = 64
DFT_S1_COLS = 4096
DFT_S2_ROWS = 8


def _params(*sem):
    return pltpu.CompilerParams(dimension_semantics=sem, vmem_limit_bytes=VMEM_LIMIT_BYTES)


def _const_spec(shape):
    nd = len(shape)
    return pl.BlockSpec(shape, lambda *_: (0,) * nd, pipeline_mode=pl.Buffered(1))


def _adaln_kernel(c_ref, w_ref, b_ref, o_ref):
    c = c_ref[...]
    s = c * jax.nn.sigmoid(c)
    o_ref[0] = jnp.dot(s, w_ref[0], preferred_element_type=F32, precision=lax.Precision.HIGHEST) + b_ref[0]


def _adaln(cond, w_mod, b_mod):
    tn = 1536
    n = 6 * D_MODEL
    return pl.pallas_call(
        _adaln_kernel,
        grid=(DEPTH, n // tn),
        in_specs=[
            pl.BlockSpec((MOD_ROWS, D_MODEL), lambda l, j: (0, 0)),
            pl.BlockSpec((1, D_MODEL, tn), lambda l, j: (l, 0, j)),
            pl.BlockSpec((1, 1, tn), lambda l, j: (l, 0, j)),
        ],
        out_specs=pl.BlockSpec((1, MOD_ROWS, tn), lambda l, j: (l, 0, j)),
        out_shape=jax.ShapeDtypeStruct((DEPTH, MOD_ROWS, n), F32),
        compiler_params=_params("arbitrary", "arbitrary"),
        name="adaln",
    )(cond, w_mod, b_mod.reshape(DEPTH, 1, n))


def _mod_spec(chunk, row0, rows_per_batch, tm):
    return pl.BlockSpec((None, None, 1, D_MODEL),
                        lambda i, *_: (row0 + (i * tm) // rows_per_batch, chunk, 0, 0))


def _rope(x, cos, sin_next, sin_prev):
    return x * cos + pltpu.roll(x, LANES - AXIS_DIM // 2, 1) * sin_next + pltpu.roll(x, AXIS_DIM // 2, 1) * sin_prev


def _proj_kernel(x_ref, g_ref, sh_ref, sc_ref, w_ref, bd_ref, qn_ref, kn_ref, cos_ref, sn_ref, sp_ref,
                 uf_ref, qg_ref, kg_ref, vg_ref, qd_ref, kd_ref, vd_ref, gt_ref, *, rope):
    x = x_ref[...]
    h = x * lax.rsqrt(jnp.mean(x * x, axis=-1, keepdims=True) + EPS) * g_ref[...]
    hb = (h * (1.0 + sc_ref[...]) + sh_ref[...]).astype(BF16)

    def mm(lo, width):
        return jnp.dot(hb, w_ref[:, lo:lo + width], preferred_element_type=F32)

    def head_norm(z, gain):
        width = z.shape[-1]
        ms = jnp.dot((z * z).astype(BF16), bd_ref[:width, :width], preferred_element_type=F32)
        return z * lax.rsqrt(ms + EPS) * gain

    def rotary(z):
        if not rope:
            return z
        cos, sn, sp = cos_ref[...], sn_ref[...], sp_ref[...]
        return jnp.concatenate(
            [_rope(z[:, c:c + LANES], cos, sn, sp) for c in range(0, z.shape[-1], LANES)], axis=-1)

    uf_ref[...] = mm(_OFF_UF, FOURIER_WIDTH).astype(uf_ref.dtype)
    qg_ref[...] = (rotary(head_norm(mm(_OFF_QG, GQA_WIDTH), qn_ref[...])) * QK_SCALE).astype(qg_ref.dtype)
    kg_ref[...] = rotary(head_norm(mm(_OFF_KG, GQA_KV_WIDTH), kn_ref[...])).astype(kg_ref.dtype)
    vg_ref[...] = mm(_OFF_VG, GQA_KV_WIDTH).astype(vg_ref.dtype)
    qd_ref[...] = (rotary(mm(_OFF_QD, DIFF_WIDTH)) * QK_SCALE).astype(qd_ref.dtype)
    kd_ref[...] = rotary(mm(_OFF_KD, DIFF_WIDTH)).astype(kd_ref.dtype)
    vd_ref[...] = mm(_OFF_VD, DIFF_WIDTH).astype(vd_ref.dtype)
    for j in range(N_BRANCHES):
        z = mm(_OFF_GT + j * D_MODEL, D_MODEL)
        gt_ref[:, j * D_MODEL:(j + 1) * D_MODEL] = jax.nn.sigmoid(z).astype(gt_ref.dtype)


def _proj(x, norm_g, mod, w_in, bd, qn, kn, rope_tabs, *, row0, rows_per_batch, rope, kv_dtype):
    t = x.shape[0]
    tm = TOKEN_TILE
    cos, sn, sp = rope_tabs
    n_pos = cos.shape[0]
    row = lambda width: pl.BlockSpec((tm, width), lambda i: (i, 0))
    tab = pl.BlockSpec((tm, LANES), lambda i: (i % (n_pos // tm), 0))
    outs = [(FOURIER_WIDTH, BF16), (GQA_WIDTH, BF16), (GQA_KV_WIDTH, kv_dtype), (GQA_KV_WIDTH, kv_dtype),
            (DIFF_WIDTH, BF16), (DIFF_WIDTH, kv_dtype), (DIFF_WIDTH, kv_dtype), (GATES_WIDTH, BF16)]
    return pl.pallas_call(
        functools.partial(_proj_kernel, rope=rope),
        grid=(t // tm,),
        in_specs=[
            row(D_MODEL),
            _const_spec((1, D_MODEL)),
            _mod_spec(0, row0, rows_per_batch, tm),
            _mod_spec(1, row0, rows_per_batch, tm),
            _const_spec((D_MODEL, IN_WIDTH)),
            _const_spec((GQA_WIDTH, GQA_WIDTH)),
            _const_spec((1, GQA_WIDTH)),
            _const_spec((1, GQA_KV_WIDTH)),
            tab, tab, tab,
        ],
        out_specs=[row(w) for w, _ in outs],
        out_shape=[jax.ShapeDtypeStruct((t, w), dt) for w, dt in outs],
        compiler_params=_params("arbitrary"),
        name="proj",
    )(x, norm_g, mod, mod, w_in, bd, qn, kn, cos, sn, sp)


def _cos_sin(n_rows, n_cols, period, scale=1.0):
    r = np.arange(n_rows, dtype=np.int64)[:, None]
    c = np.arange(n_cols, dtype=np.int64)[None, :]
    ang = 2.0 * np.pi * ((r * c) % period).astype(np.float64) / period
    return np.cos(ang) * scale, np.sin(ang) * scale


def _table(a):
    return jnp.asarray(a, F32).astype(BF16)


def _dft_small_kernel(u_ref, cs_c_ref, cs_n_ref, o_ref):
    u = u_ref[0]
    cs_c = cs_c_ref[...]
    cs_n = cs_n_ref[...]
    for g in range(FOURIER_GROUPS):
        lo = g * FOURIER_GROUP_DIM
        t = jnp.dot(u[:, lo:lo + FOURIER_GROUP_DIM], cs_c, preferred_element_type=F32).astype(BF16)
        stacked = jnp.concatenate([t[:, :FOURIER_GROUP_DIM], t[:, FOURIER_GROUP_DIM:]], axis=0)
        o_ref[0, :, lo:lo + FOURIER_GROUP_DIM] = jnp.dot(
            cs_n, stacked, preferred_element_type=F32).astype(o_ref.dtype)


def _fourier_small(u, n_batch, n_pos):
    cc, sc = _cos_sin(FOURIER_GROUP_DIM, FOURIER_GROUP_DIM, FOURIER_GROUP_DIM,
                      scale=(n_pos * FOURIER_GROUP_DIM) ** -0.5)
    cn, sn = _cos_sin(n_pos, n_pos, n_pos)
    cs_c = _table(np.concatenate([cc, sc], axis=1))
    cs_n = _table(np.concatenate([cn, -sn], axis=1))
    blk = pl.BlockSpec((1, n_pos, FOURIER_WIDTH), lambda b: (b, 0, 0))
    out = pl.pallas_call(
        _dft_small_kernel,
        grid=(n_batch,),
        in_specs=[blk, _const_spec(cs_c.shape), _const_spec(cs_n.shape)],
        out_specs=blk,
        out_shape=jax.ShapeDtypeStruct((n_batch, n_pos, FOURIER_WIDTH), BF16),
        compiler_params=_params("arbitrary"),
        name="fourier_small",
    )(u.reshape(n_batch, n_pos, FOURIER_WIDTH), cs_c, cs_n)
    return out.reshape(n_batch * n_pos, FOURIER_WIDTH)


def _dft_stage1_kernel(u_ref, f_ref, o_ref):
    o_ref[0] = jnp.dot(f_ref[...], u_ref[0], preferred_element_type=F32).astype(o_ref.dtype)


def _dft_stage2_kernel(a_ref, m_ref, cs_ref, o_ref):
    cs = cs_ref[...]
    for i in range(DFT_S2_ROWS):
        a = jnp.concatenate([a_ref[0, 0, i], a_ref[0, 1, i]], axis=0)
        b = jnp.dot(m_ref[i], a, preferred_element_type=F32).astype(BF16)
        br, bi = b[:DFT_RADIX], b[DFT_RADIX:]
        for g in range(FOURIER_GROUPS):
            lo = g * FOURIER_GROUP_DIM
            lhs = jnp.concatenate([br[:, lo:lo + FOURIER_GROUP_DIM], bi[:, lo:lo + FOURIER_GROUP_DIM]], axis=1)
            col = i * FOURIER_WIDTH + lo
            o_ref[0, :, col:col + FOURIER_GROUP_DIM] = jnp.dot(
                lhs, cs, preferred_element_type=F32).astype(o_ref.dtype)


def _fourier_large(u, n_batch, n_pos):
    r = DFT_RADIX
    assert n_pos == r * r
    wide = r * FOURIER_WIDTH
    c1, s1 = _cos_sin(r, r, r)
    f1 = _table(np.concatenate([c1, -s1], axis=0))
    k1 = np.arange(r, dtype=np.int64)[:, None, None]
    k2 = np.arange(r, dtype=np.int64)[None, :, None]
    n2 = np.arange(r, dtype=np.int64)[None, None, :]
    ang = 2.0 * np.pi * ((n2 * (r * k2 + k1)) % n_pos).astype(np.float64) / n_pos
    mr, mi = np.cos(ang), -np.sin(ang)
    m = _table(np.concatenate([np.concatenate([mr, -mi], axis=2),
                               np.concatenate([mi, mr], axis=2)], axis=1))
    cc, sc = _cos_sin(FOURIER_GROUP_DIM, FOURIER_GROUP_DIM, FOURIER_GROUP_DIM,
                      scale=(n_pos * FOURIER_GROUP_DIM) ** -0.5)
    cs = _table(np.concatenate([cc, sc], axis=0))

    a = pl.pallas_call(
        _dft_stage1_kernel,
        grid=(n_batch, wide // DFT_S1_COLS),
        in_specs=[pl.BlockSpec((1, r, DFT_S1_COLS), lambda b, j: (b, 0, j)), _const_spec(f1.shape)],
        out_specs=pl.BlockSpec((1, 2 * r, DFT_S1_COLS), lambda b, j: (b, 0, j)),
        out_shape=jax.ShapeDtypeStruct((n_batch, 2 * r, wide), BF16),
        compiler_params=_params("arbitrary", "arbitrary"),
        name="fourier_stage1",
    )(u.reshape(n_batch, r, wide), f1)

    kb = DFT_S2_ROWS
    out = pl.pallas_call(
        _dft_stage2_kernel,
        grid=(n_batch, r // kb),
        in_specs=[
            pl.BlockSpec((1, 2, kb, r, FOURIER_WIDTH), lambda b, j: (b, 0, j, 0, 0)),
            pl.BlockSpec((kb, 2 * r, 2 * r), lambda b, j: (j, 0, 0)),
            _const_spec(cs.shape),
        ],
        out_specs=pl.BlockSpec((1, r, kb * FOURIER_WIDTH), lambda b, j: (b, 0, j)),
        out_shape=jax.ShapeDtypeStruct((n_batch, r, wide), BF16),
        compiler_params=_params("arbitrary", "arbitrary"),
        name="fourier_stage2",
    )(a.reshape(n_batch, 2, r, r, FOURIER_WIDTH), m, cs)
    return out.reshape(n_batch * n_pos, FOURIER_WIDTH)


def _attn_kernel(*refs, n_groups, n_maps, k_width, v_width, has_cache, diff, lambda_init, n_chunks):
    it = iter(refs)
    q_ref, kn_ref, vn_ref = next(it), next(it), next(it)
    kc_ref = vc_ref = lam_ref = sg_ref = None
    if has_cache:
        kc_ref, vc_ref = next(it), next(it)
    if diff:
        lam_ref, sg_ref = next(it), next(it)
    o_ref, kall_ref, vt_ref = next(it), next(it), next(it)
    kc_rows = kall_ref.shape[1]
    tq = q_ref.shape[1]
    w = n_maps * tq
    kw = k_width // n_groups if diff else k_width
    vw = v_width // n_groups

    @pl.when(pl.program_id(1) == 0)
    def _():
        first_new = 0
        if has_cache:
            kall_ref[0] = kc_ref[0].astype(BF16)
            vt_ref[0] = vc_ref[0].astype(F32).T.astype(BF16)
            first_new = 1
        for c in range(n_chunks - first_new):
            rows = slice(c * kc_rows, (c + 1) * kc_rows)
            kall_ref[first_new + c] = kn_ref[0, rows, :].astype(BF16)
            vt_ref[first_new + c] = vn_ref[0, rows, :].astype(F32).T.astype(BF16)

    qt = q_ref[0].astype(F32).T.astype(BF16)
    zeros = jnp.zeros((HEAD_DIM, tq), BF16)
    if diff:
        lp = lam_ref[...]
        lam = (jnp.exp(jnp.sum(lp[0:1] * lp[1:2], axis=-1, keepdims=True))
               - jnp.exp(jnp.sum(lp[2:3] * lp[3:4], axis=-1, keepdims=True)) + lambda_init)

    outs = []
    for g in range(n_groups):
        cols = []
        for m in range(n_maps):
            qrow = (g * n_maps + m) * HEAD_DIM
            place = m if diff else g
            parts = [zeros] * (kw // HEAD_DIM)
            parts[place] = qt[qrow:qrow + HEAD_DIM]
            cols.append(jnp.concatenate(parts, axis=0))
        rhs = jnp.concatenate(cols, axis=1)
        k_lo = g * kw if diff else 0
        v_lo = g * vw

        def body(c, carry):
            m_run, l_run, acc = carry
            kblk = kall_ref[c, :, k_lo:k_lo + kw]
            s = jnp.dot(kblk, rhs, preferred_element_type=F32)
            m_new = jnp.maximum(m_run, jnp.max(s, axis=0, keepdims=True))
            alpha = jnp.exp(m_run - m_new)
            p = jnp.exp(s - m_new)
            l_new = alpha * l_run + jnp.sum(p, axis=0, keepdims=True)
            vblk = vt_ref[c, v_lo:v_lo + vw, :]
            acc_new = alpha * acc + jnp.dot(vblk, p.astype(BF16), preferred_element_type=F32)
            return m_new, l_new, acc_new

        init = (jnp.full((1, w), -1e30, F32), jnp.zeros((1, w), F32), jnp.zeros((vw, w), F32))
        _, l_fin, acc = lax.fori_loop(0, n_chunks, body, init)
        o = acc / l_fin
        if diff:
            o = o[:, :tq] - lam * o[:, tq:]
            o = o * lax.rsqrt(jnp.mean(o * o, axis=0, keepdims=True) + EPS) * sg_ref[...] * (1.0 - lambda_init)
            outs.append(o)
        else:
            outs.extend(o[:, m * tq:(m + 1) * tq] for m in range(n_maps))
    o_ref[0] = jnp.concatenate(outs, axis=0).T.astype(o_ref.dtype)


def _attention(q, k_new, v_new, cache, extra, *, n_batch, n_pos, diff, lambda_init):
    k_width = k_new.shape[-1]
    v_width = v_new.shape[-1]
    tq = min(ATTN_Q_TILE, n_pos)
    kc_rows = min(ATTN_K_CHUNK, n_pos)
    has_cache = cache is not None
    n_keys = n_pos + (cache[0].shape[1] if has_cache else 0)
    if has_cache:
        assert cache[0].shape[1] == kc_rows
    n_chunks = n_keys // kc_rows
    n_groups = DIFF_HEADS if diff else GQA_KV_HEADS
    n_maps = 2 if diff else GQA_HEADS // GQA_KV_HEADS
    q_w = q.shape[-1]

    full = lambda width: pl.BlockSpec((1, n_pos, width), lambda b, i: (b, 0, 0))
    in_specs = [pl.BlockSpec((1, tq, q_w), lambda b, i: (b, i, 0)), full(k_width), full(v_width)]
    args = [q.reshape(n_batch, n_pos, q_w), k_new.reshape(n_batch, n_pos, k_width),
            v_new.reshape(n_batch, n_pos, v_width)]
    if has_cache:
        past = cache[0].shape[1]
        in_specs += [pl.BlockSpec((1, past, k_width), lambda b, i: (b, 0, 0)),
                     pl.BlockSpec((1, past, v_width), lambda b, i: (b, 0, 0))]
        args += list(cache)
    if diff:
        in_specs += [_const_spec(extra[0].shape), _const_spec(extra[1].shape)]
        args += list(extra)
    out = pl.pallas_call(
        functools.partial(_attn_kernel, n_groups=n_groups, n_maps=n_maps, k_width=k_width, v_width=v_width,
                          has_cache=has_cache, diff=diff, lambda_init=lambda_init, n_chunks=n_chunks),
        grid=(n_batch, n_pos // tq),
        in_specs=in_specs,
        out_specs=pl.BlockSpec((1, tq, q_w), lambda b, i: (b, i, 0)),
        out_shape=jax.ShapeDtypeStruct((n_batch, n_pos, q_w), BF16),
        scratch_shapes=[pltpu.VMEM((n_chunks, kc_rows, k_width), BF16),
                        pltpu.VMEM((n_chunks, v_width, kc_rows), BF16)],
        compiler_params=_params("arbitrary", "arbitrary"),
        name="diff_attention" if diff else "gqa_attention",
    )(*args)
    return out.reshape(n_batch * n_pos, q_w)


def _merge_kernel(f_ref, og_ref, od_ref, gt_ref, x_ref, wf_ref, wg_ref, wd_ref, wo_ref,
                  g1_ref, n2_ref, sh2_ref, sc2_ref, xo_ref, h2_ref):
    def branch(j, a_ref, w_ref):
        gate = gt_ref[:, j * D_MODEL:(j + 1) * D_MODEL].astype(F32)
        return gate * jnp.dot(a_ref[...], w_ref[...], preferred_element_type=F32)

    merged = branch(0, f_ref, wf_ref) + branch(1, og_ref, wg_ref) + branch(2, od_ref, wd_ref)
    x = x_ref[...] + g1_ref[...] * jnp.dot(merged.astype(BF16), wo_ref[...], preferred_element_type=F32)
    xo_ref[...] = x
    h = x * lax.rsqrt(jnp.mean(x * x, axis=-1, keepdims=True) + EPS) * n2_ref[...]
    h2_ref[...] = (h * (1.0 + sc2_ref[...]) + sh2_ref[...]).astype(h2_ref.dtype)


def _merge(f, og, od, gates, x, w_f, w_go, w_do, w_out, mod, norm2_g, *, row0, rows_per_batch):
    t = x.shape[0]
    tm = TOKEN_TILE
    row = lambda width: pl.BlockSpec((tm, width), lambda i: (i, 0))
    return pl.pallas_call(
        _merge_kernel,
        grid=(t // tm,),
        in_specs=[
            row(FOURIER_WIDTH), row(GQA_WIDTH), row(DIFF_WIDTH), row(GATES_WIDTH), row(D_MODEL),
            _const_spec(w_f.shape), _const_spec(w_go.shape), _const_spec(w_do.shape), _const_spec(w_out.shape),
            _mod_spec(2, row0, rows_per_batch, tm),
            _const_spec((1, D_MODEL)),
            _mod_spec(3, row0, rows_per_batch, tm),
            _mod_spec(4, row0, rows_per_batch, tm),
        ],
        out_specs=[row(D_MODEL), row(D_MODEL)],
        out_shape=[jax.ShapeDtypeStruct((t, D_MODEL), F32), jax.ShapeDtypeStruct((t, D_MODEL), BF16)],
        compiler_params=_params("arbitrary"),
        name="merge",
    )(f, og, od, gates, x, w_f, w_go, w_do, w_out, mod, norm2_g, mod, mod)


def _routing_weights(scores, sel):
    rows = [sel[e:e + 1] for e in range(N_EXPERTS)]
    group_score = []
    for g in range(N_EXPERT_GROUPS):
        a, b, c, d = rows[EXPERTS_PER_GROUP * g:EXPERTS_PER_GROUP * (g + 1)]
        hi1, lo1, hi2, lo2 = jnp.maximum(a, b), jnp.minimum(a, b), jnp.maximum(c, d), jnp.minimum(c, d)
        group_score.append(jnp.maximum(hi1, hi2) + jnp.maximum(jnp.minimum(hi1, hi2), jnp.maximum(lo1, lo2)))
    best, best_idx = group_score[0], jnp.zeros_like(group_score[0], dtype=jnp.int32)
    for g in range(1, N_EXPERT_GROUPS):
        better = group_score[g] > best
        best = jnp.where(better, group_score[g], best)
        best_idx = jnp.where(better, g, best_idx)
    picked = []
    for e in range(N_EXPERTS):
        g = e // EXPERTS_PER_GROUP
        rank = jnp.zeros_like(best_idx)
        for j in range(EXPERTS_PER_GROUP * g, EXPERTS_PER_GROUP * (g + 1)):
            if j != e:
                ahead = (rows[j] > rows[e]) | ((rows[j] == rows[e]) & (j < e))
                rank = rank + ahead.astype(jnp.int32)
        picked.append((best_idx == g) & (rank < 2))
    weight = [jnp.where(picked[e], scores[e:e + 1], 0.0) for e in range(N_EXPERTS)]
    total = weight[0]
    for e in range(1, N_EXPERTS):
        total = total + weight[e]
    row_id = lax.broadcasted_iota(jnp.int32, scores.shape, 0)
    comb = jnp.zeros_like(scores)
    for e in range(N_EXPERTS):
        comb = jnp.where(row_id == e, weight[e] / total, comb)
    return comb


def _moe_kernel(h_ref, x_ref, wr_ref, rb_ref, wgu_ref, wd_ref, g2_ref, fg_ref, o_ref, cb_ref, acc_ref, *, final):
    e = pl.program_id(1)
    tm = h_ref.shape[0]

    @pl.when(e == 0)
    def _():
        logits = lax.dot_general(wr_ref[...], h_ref[...], (((1,), (1,)), ((), ())),
                                 preferred_element_type=F32)
        scores = jax.nn.sigmoid(logits)
        comb_t = _routing_weights(scores, scores + rb_ref[...])
        comb = jnp.concatenate([comb_t, jnp.zeros((LANES - N_EXPERTS, tm), F32)], axis=0).T
        for j in range(N_EXPERTS):
            cb_ref[j] = jnp.broadcast_to(comb[:, j:j + 1], (tm, LANES))
        acc_ref[...] = jnp.zeros_like(acc_ref)

    a = jnp.dot(h_ref[...], wgu_ref[0], preferred_element_type=F32)
    gate, up = a[:, :D_EXPERT], a[:, D_EXPERT:]
    cb = cb_ref[e]
    act = gate * jax.nn.sigmoid(gate) * up * jnp.concatenate([cb] * (D_EXPERT // LANES), axis=1)
    acc_ref[...] += jnp.dot(act.astype(BF16), wd_ref[0], preferred_element_type=F32)

    @pl.when(e == N_EXPERTS - 1)
    def _():
        y = x_ref[...] + g2_ref[...] * acc_ref[...]
        if final:
            y = y * lax.rsqrt(jnp.mean(y * y, axis=-1, keepdims=True) + EPS) * fg_ref[...]
        o_ref[...] = y


def _moe(h2, x, w_router_t, router_bias, w_gu, w_d, mod, final_g, *, row0, rows_per_batch, final):
    t = x.shape[0]
    tm = TOKEN_TILE
    row = lambda width: pl.BlockSpec((tm, width), lambda i, e: (i, 0))
    return pl.pallas_call(
        functools.partial(_moe_kernel, final=final),
        grid=(t // tm, N_EXPERTS),
        in_specs=[
            row(D_MODEL), row(D_MODEL),
            _const_spec(w_router_t.shape), _const_spec(router_bias.shape),
            pl.BlockSpec((1, D_MODEL, 2 * D_EXPERT), lambda i, e: (e, 0, 0)),
            pl.BlockSpec((1, D_EXPERT, D_MODEL), lambda i, e: (e, 0, 0)),
            _mod_spec(5, row0, rows_per_batch, tm),
            _const_spec((1, D_MODEL)),
        ],
        out_specs=row(D_MODEL),
        out_shape=jax.ShapeDtypeStruct((t, D_MODEL), F32),
        scratch_shapes=[pltpu.VMEM((N_EXPERTS, tm, LANES), F32), pltpu.VMEM((tm, D_MODEL), F32)],
        compiler_params=_params("arbitrary", "arbitrary"),
        name="moe",
    )(h2, x, w_router_t, router_bias, w_gu, w_d, mod, final_g)


def _rope_tables(n_tokens):
    rows = n_tokens // GRID_W
    row = np.repeat(np.arange(rows), GRID_W).astype(np.float64)
    col = np.tile(np.arange(GRID_W), rows).astype(np.float64)
    inv = ROPE_THETA ** (-(np.arange(AXIS_DIM // 2, dtype=np.float64) * 2.0 / AXIS_DIM))
    ang_r = row[:, None] * inv[None, :]
    ang_c = col[:, None] * inv[None, :]
    ang = np.concatenate([ang_r, ang_r, ang_c, ang_c], axis=-1)
    ang = np.concatenate([ang, ang], axis=-1)
    first_half = (np.arange(LANES) % AXIS_DIM) < AXIS_DIM // 2
    sin = np.sin(ang)
    sin_next = np.where(first_half[None, :], -sin, 0.0)
    sin_prev = np.where(first_half[None, :], 0.0, sin)
    return tuple(jnp.asarray(a, F32) for a in (np.cos(ang), sin_next, sin_prev))


def kernel(x_prompt, x_sample, cache_gqa_k, cache_gqa_v, cache_diff_k, cache_diff_v, c, c_ctx, norm1_g, w_mod,
           b_mod, w_in, gqa_q_norm, gqa_k_norm, diff_lambda, diff_subln_g, w_fourier, w_gqa_o, w_diff_o, w_out,
           norm2_g, w_router, router_bias, w_e_gate, w_e_up, w_e_down, final_norm_g):
    n_ctx_b, n_ctx, _ = x_prompt.shape
    n_lat_b, n_lat, _ = x_sample.shape
    past = cache_gqa_k.shape[2]
    assert n_lat_b + 1 <= MOD_ROWS

    w_in_b = w_in.astype(BF16)
    w_f_b, w_go_b, w_do_b, w_out_b = (w.astype(BF16) for w in (w_fourier, w_gqa_o, w_diff_o, w_out))
    w_gu_b = jnp.concatenate([w_e_gate, w_e_up], axis=-1).astype(BF16)
    w_d_b = w_e_down.astype(BF16)
    w_router_t = w_router.T.astype(BF16)
    rbias = router_bias.reshape(N_EXPERTS, 1)
    head_id = np.arange(GQA_WIDTH) // HEAD_DIM
    bd = jnp.asarray((head_id[:, None] == head_id[None, :]) / HEAD_DIM, BF16)
    final_g = final_norm_g.reshape(1, D_MODEL)

    cond = jnp.concatenate([c_ctx[None, :], c, jnp.zeros((MOD_ROWS - 1 - n_lat_b, D_MODEL), F32)], axis=0)
    mod = _adaln(cond, w_mod, b_mod).reshape(DEPTH, MOD_ROWS, 6, 1, D_MODEL)

    lat_rope = _rope_tables(n_lat)
    no_rope = tuple(jnp.zeros((TOKEN_TILE, LANES), F32) for _ in range(3))

    def run_pass(x, n_batch, n_pos, row0, rope_tabs, rope, caches, kv_dtype):
        t = n_batch * n_pos
        rows_per_batch = n_pos if row0 else t
        x = x.reshape(t, D_MODEL)
        kv_out = []
        for l in range(DEPTH):
            lambda_init = 0.8 - 0.6 * math.exp(-0.3 * l)
            uf, qg, kg, vg, qd, kd, vd, gates = _proj(
                x, norm1_g[l].reshape(1, D_MODEL), mod[l], w_in_b[l], bd,
                jnp.tile(gqa_q_norm[l], GQA_HEADS).reshape(1, GQA_WIDTH),
                jnp.tile(gqa_k_norm[l], GQA_KV_HEADS).reshape(1, GQA_KV_WIDTH),
                rope_tabs, row0=row0, rows_per_batch=rows_per_batch, rope=rope, kv_dtype=kv_dtype)
            kv_out.append((kg, vg, kd, vd))
            four = _fourier_large(uf, n_batch, n_pos) if n_pos == DFT_RADIX ** 2 else _fourier_small(uf, n_batch, n_pos)
            cg = cd = None
            if caches is not None:
                cg = (caches[0][:, l].reshape(n_batch, past, GQA_KV_WIDTH),
                      caches[1][:, l].reshape(n_batch, past, GQA_KV_WIDTH))
                cd = (caches[2][:, l].reshape(n_batch, past, DIFF_WIDTH),
                      caches[3][:, l].reshape(n_batch, past, DIFF_WIDTH))
            og = _attention(qg, kg, vg, cg, None, n_batch=n_batch, n_pos=n_pos, diff=False, lambda_init=0.0)
            od = _attention(qd, kd, vd, cd,
                            (diff_lambda[l], diff_subln_g[l].reshape(2 * HEAD_DIM, 1)),
                            n_batch=n_batch, n_pos=n_pos, diff=True, lambda_init=lambda_init)
            x, h2 = _merge(four, og, od, gates, x, w_f_b[l], w_go_b[l], w_do_b[l], w_out_b[l], mod[l],
                           norm2_g[l].reshape(1, D_MODEL), row0=row0, rows_per_batch=rows_per_batch)
            x = _moe(h2, x, w_router_t, rbias, w_gu_b[l], w_d_b[l], mod[l], final_g,
                     row0=row0, rows_per_batch=rows_per_batch, final=(l == DEPTH - 1))
        return x.reshape(n_batch, n_pos, D_MODEL), kv_out

    y_prompt, kv = run_pass(x_prompt, n_ctx_b, n_ctx, 0, no_rope, False, None, F32)
    y_sample, _ = run_pass(x_sample, n_lat_b, n_lat, 1, lat_rope, True,
                           (cache_gqa_k, cache_gqa_v, cache_diff_k, cache_diff_v), BF16)

    def stack(idx, shape):
        return jnp.stack([kv[l][idx].reshape((n_ctx_b, n_ctx) + shape) for l in range(DEPTH)], axis=1)

    return (y_prompt, y_sample,
            stack(0, (GQA_KV_HEADS, HEAD_DIM)), stack(1, (GQA_KV_HEADS, HEAD_DIM)),
            stack(2, (DIFF_HEADS, 2, HEAD_DIM)), stack(3, (DIFF_HEADS, 2 * HEAD_DIM)))
```

```python
import functools
import math

import jax
import jax.numpy as jnp
import numpy as np
from jax import lax
from jax.experimental import pallas as pl
from jax.experimental.pallas import tpu as pltpu

F32 = jnp.float32
BF16 = jnp.bfloat16

D_MODEL = 1024
DEPTH = 2
GRID_W = 64
HEAD_DIM = 64
AXIS_DIM = HEAD_DIM // 2
GQA_HEADS = 8
GQA_KV_HEADS = 2
DIFF_HEADS = 4
FOURIER_GROUPS = 4
FOURIER_GROUP_DIM = 128
FOURIER_WIDTH = FOURIER_GROUPS * FOURIER_GROUP_DIM
GQA_WIDTH = GQA_HEADS * HEAD_DIM
GQA_KV_WIDTH = GQA_KV_HEADS * HEAD_DIM
DIFF_WIDTH = DIFF_HEADS * 2 * HEAD_DIM
N_BRANCHES = 3
GATES_WIDTH = N_BRANCHES * D_MODEL
N_EXPERTS = 16
N_EXPERT_GROUPS = 4
EXPERTS_PER_GROUP = N_EXPERTS // N_EXPERT_GROUPS
D_EXPERT = 256
ROPE_THETA = 10000.0
EPS = 1e-6
QK_SCALE = HEAD_DIM ** -0.5 * math.log2(math.e)

_OFF_UF = 0
_OFF_QG = _OFF_UF + FOURIER_WIDTH
_OFF_KG = _OFF_QG + GQA_WIDTH
_OFF_VG = _OFF_KG + GQA_KV_WIDTH
_OFF_QD = _OFF_VG + GQA_KV_WIDTH
_OFF_KD = _OFF_QD + DIFF_WIDTH
_OFF_VD = _OFF_KD + DIFF_WIDTH
_OFF_GT = _OFF_VD + DIFF_WIDTH
IN_WIDTH = _OFF_GT + GATES_WIDTH

LANES = 128
SUBLANES = 8
VMEM_LIMIT_BYTES = 56 * 1024 * 1024

MOD_ROWS = 8
TOKEN_TILE = 512
ATTN_Q_TILE = 256
ATTN_K_CHUNK = 512
ATTN_ONES_ROWS = 16
ROUTE_ROWS = 16
POS_LANE = N_EXPERTS
MOE_BLOCK = 128
DFT_RADIX = 64
DFT_S1_COLS = 4096
DFT_S2_ROWS = 8


def _params(*sem):
    return pltpu.CompilerParams(dimension_semantics=sem, vmem_limit_bytes=VMEM_LIMIT_BYTES)


def _const_spec(shape):
    nd = len(shape)
    return pl.BlockSpec(shape, lambda *_: (0,) * nd, pipeline_mode=pl.Buffered(1))


def _adaln_kernel(c_ref, w_ref, b_ref, o_ref):
    c = c_ref[...]
    s = c * jax.nn.sigmoid(c)
    o_ref[0] = jnp.dot(s, w_ref[0], preferred_element_type=F32, precision=lax.Precision.HIGHEST) + b_ref[0]


def _adaln(cond, w_mod, b_mod):
    tn = 1536
    n = 6 * D_MODEL
    return pl.pallas_call(
        _adaln_kernel,
        grid=(DEPTH, n // tn),
        in_specs=[
            pl.BlockSpec((MOD_ROWS, D_MODEL), lambda l, j: (0, 0)),
            pl.BlockSpec((1, D_MODEL, tn), lambda l, j: (l, 0, j)),
            pl.BlockSpec((1, 1, tn), lambda l, j: (l, 0, j)),
        ],
        out_specs=pl.BlockSpec((1, MOD_ROWS, tn), lambda l, j: (l, 0, j)),
        out_shape=jax.ShapeDtypeStruct((DEPTH, MOD_ROWS, n), F32),
        compiler_params=_params("arbitrary", "arbitrary"),
        name="adaln",
    )(cond, w_mod, b_mod.reshape(DEPTH, 1, n))


def _mod_spec(chunk, row0, rows_per_batch, tm):
    return pl.BlockSpec((None, None, 1, D_MODEL),
                        lambda i, *_: (row0 + (i * tm) // rows_per_batch, chunk, 0, 0))


def _rope(x, cos, sin_next, sin_prev):
    return x * cos + pltpu.roll(x, LANES - AXIS_DIM // 2, 1) * sin_next + pltpu.roll(x, AXIS_DIM // 2, 1) * sin_prev


def _proj_kernel(x_ref, g_ref, sh_ref, sc_ref, w_ref, bd_ref, qn_ref, kn_ref, cos_ref, sn_ref, sp_ref,
                 uf_ref, qg_ref, kg_ref, vg_ref, qd_ref, kd_ref, vd_ref, gt_ref, *, rope):
    x = x_ref[...]
    h = x * lax.rsqrt(jnp.mean(x * x, axis=-1, keepdims=True) + EPS) * g_ref[...]
    hb = (h * (1.0 + sc_ref[...]) + sh_ref[...]).astype(BF16)

    def mm(lo, width):
        return jnp.dot(hb, w_ref[:, lo:lo + width], preferred_element_type=F32)

    def head_norm(z, gain):
        width = z.shape[-1]
        ms = jnp.dot((z * z).astype(BF16), bd_ref[:width, :width], preferred_element_type=F32)
        return z * lax.rsqrt(ms + EPS) * gain

    def rotary(z):
        if not rope:
            return z
        cos, sn, sp = cos_ref[...], sn_ref[...], sp_ref[...]
        return jnp.concatenate(
            [_rope(z[:, c:c + LANES], cos, sn, sp) for c in range(0, z.shape[-1], LANES)], axis=-1)

    uf_ref[...] = mm(_OFF_UF, FOURIER_WIDTH).astype(uf_ref.dtype)
    qg_ref[...] = (rotary(head_norm(mm(_OFF_QG, GQA_WIDTH), qn_ref[...])) * QK_SCALE).astype(qg_ref.dtype)
    kg_ref[...] = rotary(head_norm(mm(_OFF_KG, GQA_KV_WIDTH), kn_ref[...])).astype(kg_ref.dtype)
    vg_ref[...] = mm(_OFF_VG, GQA_KV_WIDTH).astype(vg_ref.dtype)
    qd_ref[...] = (rotary(mm(_OFF_QD, DIFF_WIDTH)) * QK_SCALE).astype(qd_ref.dtype)
    kd_ref[...] = rotary(mm(_OFF_KD, DIFF_WIDTH)).astype(kd_ref.dtype)
    vd_ref[...] = mm(_OFF_VD, DIFF_WIDTH).astype(vd_ref.dtype)
    for j in range(N_BRANCHES):
        z = mm(_OFF_GT + j * D_MODEL, D_MODEL)
        gt_ref[:, j * D_MODEL:(j + 1) * D_MODEL] = jax.nn.sigmoid(z).astype(gt_ref.dtype)


def _proj(x, norm_g, mod, w_in, bd, qn, kn, rope_tabs, *, row0, rows_per_batch, rope, kv_dtype):
    t = x.shape[0]
    tm = TOKEN_TILE
    cos, sn, sp = rope_tabs
    n_pos = cos.shape[0]
    row = lambda width: pl.BlockSpec((tm, width), lambda i: (i, 0))
    tab = pl.BlockSpec((tm, LANES), lambda i: (i % (n_pos // tm), 0))
    outs = [(FOURIER_WIDTH, BF16), (GQA_WIDTH, BF16), (GQA_KV_WIDTH, kv_dtype), (GQA_KV_WIDTH, kv_dtype),
            (DIFF_WIDTH, BF16), (DIFF_WIDTH, kv_dtype), (DIFF_WIDTH, kv_dtype), (GATES_WIDTH, BF16)]
    return pl.pallas_call(
        functools.partial(_proj_kernel, rope=rope),
        grid=(t // tm,),
        in_specs=[
            row(D_MODEL),
            _const_spec((1, D_MODEL)),
            _mod_spec(0, row0, rows_per_batch, tm),
            _mod_spec(1, row0, rows_per_batch, tm),
            _const_spec((D_MODEL, IN_WIDTH)),
            _const_spec((GQA_WIDTH, GQA_WIDTH)),
            _const_spec((1, GQA_WIDTH)),
            _const_spec((1, GQA_KV_WIDTH)),
            tab, tab, tab,
        ],
        out_specs=[row(w) for w, _ in outs],
        out_shape=[jax.ShapeDtypeStruct((t, w), dt) for w, dt in outs],
        compiler_params=_params("arbitrary"),
        name="proj",
    )(x, norm_g, mod, mod, w_in, bd, qn, kn, cos, sn, sp)


def _cos_sin(n_rows, n_cols, period, scale=1.0):
    r = np.arange(n_rows, dtype=np.int64)[:, None]
    c = np.arange(n_cols, dtype=np.int64)[None, :]
    ang = 2.0 * np.pi * ((r * c) % period).astype(np.float64) / period
    return np.cos(ang) * scale, np.sin(ang) * scale


def _table(a):
    return jnp.asarray(a, F32).astype(BF16)


def _dft_small_kernel(u_ref, cs_c_ref, cs_n_ref, o_ref):
    u = u_ref[0]
    cs_c = cs_c_ref[...]
    cs_n = cs_n_ref[...]
    for g in range(FOURIER_GROUPS):
        lo = g * FOURIER_GROUP_DIM
        t = jnp.dot(u[:, lo:lo + FOURIER_GROUP_DIM], cs_c, preferred_element_type=F32).astype(BF16)
        stacked = jnp.concatenate([t[:, :FOURIER_GROUP_DIM], t[:, FOURIER_GROUP_DIM:]], axis=0)
        o_ref[0, :, lo:lo + FOURIER_GROUP_DIM] = jnp.dot(
            cs_n, stacked, preferred_element_type=F32).astype(o_ref.dtype)


def _fourier_small(u, n_batch, n_pos):
    cc, sc = _cos_sin(FOURIER_GROUP_DIM, FOURIER_GROUP_DIM, FOURIER_GROUP_DIM,
                      scale=(n_pos * FOURIER_GROUP_DIM) ** -0.5)
    cn, sn = _cos_sin(n_pos, n_pos, n_pos)
    cs_c = _table(np.concatenate([cc, sc], axis=1))
    cs_n = _table(np.concatenate([cn, -sn], axis=1))
    blk = pl.BlockSpec((1, n_pos, FOURIER_WIDTH), lambda b: (b, 0, 0))
    out = pl.pallas_call(
        _dft_small_kernel,
        grid=(n_batch,),
        in_specs=[blk, _const_spec(cs_c.shape), _const_spec(cs_n.shape)],
        out_specs=blk,
        out_shape=jax.ShapeDtypeStruct((n_batch, n_pos, FOURIER_WIDTH), BF16),
        compiler_params=_params("arbitrary"),
        name="fourier_small",
    )(u.reshape(n_batch, n_pos, FOURIER_WIDTH), cs_c, cs_n)
    return out.reshape(n_batch * n_pos, FOURIER_WIDTH)


def _dft_stage1_kernel(u_ref, f_ref, o_ref):
    o_ref[0] = jnp.dot(f_ref[...], u_ref[0], preferred_element_type=F32).astype(o_ref.dtype)


def _dft_stage2_kernel(a_ref, m_ref, cs_ref, o_ref):
    cs = cs_ref[...]
    for i in range(DFT_S2_ROWS):
        a = jnp.concatenate([a_ref[0, 0, i], a_ref[0, 1, i]], axis=0)
        b = jnp.dot(m_ref[i], a, preferred_element_type=F32).astype(BF16)
        br, bi = b[:DFT_RADIX], b[DFT_RADIX:]
        for g in range(FOURIER_GROUPS):
            lo = g * FOURIER_GROUP_DIM
            lhs = jnp.concatenate([br[:, lo:lo + FOURIER_GROUP_DIM], bi[:, lo:lo + FOURIER_GROUP_DIM]], axis=1)
            col = i * FOURIER_WIDTH + lo
            o_ref[0, :, col:col + FOURIER_GROUP_DIM] = jnp.dot(
                lhs, cs, preferred_element_type=F32).astype(o_ref.dtype)


def _fourier_large(u, n_batch, n_pos):
    r = DFT_RADIX
    assert n_pos == r * r
    wide = r * FOURIER_WIDTH
    c1, s1 = _cos_sin(r, r, r)
    f1 = _table(np.concatenate([c1, -s1], axis=0))
    k1 = np.arange(r, dtype=np.int64)[:, None, None]
    k2 = np.arange(r, dtype=np.int64)[None, :, None]
    n2 = np.arange(r, dtype=np.int64)[None, None, :]
    ang = 2.0 * np.pi * ((n2 * (r * k2 + k1)) % n_pos).astype(np.float64) / n_pos
    mr, mi = np.cos(ang), -np.sin(ang)
    m = _table(np.concatenate([np.concatenate([mr, -mi], axis=2),
                               np.concatenate([mi, mr], axis=2)], axis=1))
    cc, sc = _cos_sin(FOURIER_GROUP_DIM, FOURIER_GROUP_DIM, FOURIER_GROUP_DIM,
                      scale=(n_pos * FOURIER_GROUP_DIM) ** -0.5)
    cs = _table(np.concatenate([cc, sc], axis=0))

    a = pl.pallas_call(
        _dft_stage1_kernel,
        grid=(n_batch, wide // DFT_S1_COLS),
        in_specs=[pl.BlockSpec((1, r, DFT_S1_COLS), lambda b, j: (b, 0, j)), _const_spec(f1.shape)],
        out_specs=pl.BlockSpec((1, 2 * r, DFT_S1_COLS), lambda b, j: (b, 0, j)),
        out_shape=jax.ShapeDtypeStruct((n_batch, 2 * r, wide), BF16),
        compiler_params=_params("arbitrary", "arbitrary"),
        name="fourier_stage1",
    )(u.reshape(n_batch, r, wide), f1)

    kb = DFT_S2_ROWS
    out = pl.pallas_call(
        _dft_stage2_kernel,
        grid=(n_batch, r // kb),
        in_specs=[
            pl.BlockSpec((1, 2, kb, r, FOURIER_WIDTH), lambda b, j: (b, 0, j, 0, 0)),
            pl.BlockSpec((kb, 2 * r, 2 * r), lambda b, j: (j, 0, 0)),
            _const_spec(cs.shape),
        ],
        out_specs=pl.BlockSpec((1, r, kb * FOURIER_WIDTH), lambda b, j: (b, 0, j)),
        out_shape=jax.ShapeDtypeStruct((n_batch, r, wide), BF16),
        compiler_params=_params("arbitrary", "arbitrary"),
        name="fourier_stage2",
    )(a.reshape(n_batch, 2, r, r, FOURIER_WIDTH), m, cs)
    return out.reshape(n_batch * n_pos, FOURIER_WIDTH)


def _attn_kernel(*refs, n_groups, n_maps, k_width, v_width, has_cache, diff, lambda_init, n_chunks):
    it = iter(refs)
    q_ref, kn_ref, vn_ref = next(it), next(it), next(it)
    kc_ref = vc_ref = lam_ref = sg_ref = None
    if has_cache:
        kc_ref, vc_ref = next(it), next(it)
    if diff:
        lam_ref, sg_ref = next(it), next(it)
    o_ref, kall_ref, vt_ref, rhs_ref, s0_ref, s1_ref = (next(it) for _ in range(6))
    kc_rows = kall_ref.shape[1]
    tq = q_ref.shape[1]
    w = n_maps * tq
    kw = k_width // n_groups if diff else k_width
    vw = v_width // n_groups

    @pl.when(pl.program_id(1) == 0)
    def _():
        ones = jnp.ones((ATTN_ONES_ROWS, kc_rows), BF16)

        def put(c, k_blk, v_blk):
            kall_ref[c] = k_blk.astype(BF16)
            v_t = v_blk.astype(F32).T.astype(BF16)
            for g in range(n_groups):
                vt_ref[c, g, :vw, :] = v_t[g * vw:(g + 1) * vw]
                vt_ref[c, g, vw:, :] = ones

        first_new = 0
        if has_cache:
            put(0, kc_ref[0], vc_ref[0])
            first_new = 1
        for c in range(n_chunks - first_new):
            rows = slice(c * kc_rows, (c + 1) * kc_rows)
            put(first_new + c, kn_ref[0, rows, :], vn_ref[0, rows, :])

    qt = q_ref[0].astype(F32).T.astype(BF16)
    zeros = jnp.zeros((HEAD_DIM, tq), BF16)
    for g in range(n_groups):
        cols = []
        for m in range(n_maps):
            qrow = (g * n_maps + m) * HEAD_DIM
            parts = [zeros] * (kw // HEAD_DIM)
            parts[m if diff else g] = qt[qrow:qrow + HEAD_DIM]
            cols.append(jnp.concatenate(parts, axis=0))
        rhs_ref[g] = jnp.concatenate(cols, axis=1)

    s_refs = (s0_ref, s1_ref)

    def scores(c, slot):
        for g in range(n_groups):
            k_lo = g * kw if diff else 0
            s_refs[slot][g] = jnp.dot(kall_ref[c, :, k_lo:k_lo + kw], rhs_ref[g], preferred_element_type=F32)

    def absorb(c, slot, carry):
        new = []
        for g in range(n_groups):
            m_run, acc = carry[g]
            s = s_refs[slot][g]
            m_new = jnp.maximum(m_run, jnp.max(s, axis=0, keepdims=True))
            p = jnp.exp2(s - m_new).astype(BF16)
            acc = jnp.exp2(m_run - m_new) * acc + jnp.dot(vt_ref[c, g], p, preferred_element_type=F32)
            new.append((m_new, acc))
        return tuple(new)

    def step(c, slot, carry):
        scores(c + 1, 1 - slot)
        return absorb(c, slot, carry)

    def body(j, carry):
        return step(2 * j + 1, 1, step(2 * j, 0, carry))

    carry = tuple((jnp.full((1, w), -1e30, F32), jnp.zeros((vw + ATTN_ONES_ROWS, w), F32))
                  for _ in range(n_groups))
    scores(0, 0)
    carry = lax.fori_loop(0, (n_chunks - 1) // 2, body, carry)
    for c in range(2 * ((n_chunks - 1) // 2), n_chunks - 1):
        carry = step(c, c % 2, carry)
    fin = absorb(n_chunks - 1, (n_chunks - 1) % 2, carry)

    if diff:
        lp = lam_ref[...]
        lam = (jnp.exp(jnp.sum(lp[0:1] * lp[1:2], axis=-1, keepdims=True))
               - jnp.exp(jnp.sum(lp[2:3] * lp[3:4], axis=-1, keepdims=True)) + lambda_init)
    outs = []
    for g in range(n_groups):
        acc = fin[g][1]
        o = acc[:vw] / acc[vw:vw + 1]
        if diff:
            o = o[:, :tq] - lam * o[:, tq:]
            o = o * lax.rsqrt(jnp.mean(o * o, axis=0, keepdims=True) + EPS) * sg_ref[...] * (1.0 - lambda_init)
            outs.append(o)
        else:
            outs.extend(o[:, m * tq:(m + 1) * tq] for m in range(n_maps))
    o_ref[0] = jnp.concatenate(outs, axis=0).T.astype(o_ref.dtype)


def _attention(q, k_new, v_new, cache, extra, *, n_batch, n_pos, diff, lambda_init):
    k_width = k_new.shape[-1]
    v_width = v_new.shape[-1]
    tq = min(ATTN_Q_TILE, n_pos)
    kc_rows = min(ATTN_K_CHUNK, n_pos)
    has_cache = cache is not None
    n_keys = n_pos + (cache[0].shape[1] if has_cache else 0)
    if has_cache:
        assert cache[0].shape[1] == kc_rows
    n_chunks = n_keys // kc_rows
    n_groups = DIFF_HEADS if diff else GQA_KV_HEADS
    n_maps = 2 if diff else GQA_HEADS // GQA_KV_HEADS
    q_w = q.shape[-1]

    full = lambda width: pl.BlockSpec((1, n_pos, width), lambda b, i: (b, 0, 0))
    in_specs = [pl.BlockSpec((1, tq, q_w), lambda b, i: (b, i, 0)), full(k_width), full(v_width)]
    args = [q.reshape(n_batch, n_pos, q_w), k_new.reshape(n_batch, n_pos, k_width),
            v_new.reshape(n_batch, n_pos, v_width)]
    if has_cache:
        past = cache[0].shape[1]
        in_specs += [pl.BlockSpec((1, past, k_width), lambda b, i: (b, 0, 0)),
                     pl.BlockSpec((1, past, v_width), lambda b, i: (b, 0, 0))]
        args += list(cache)
    if diff:
        in_specs += [_const_spec(extra[0].shape), _const_spec(extra[1].shape)]
        args += list(extra)
    out = pl.pallas_call(
        functools.partial(_attn_kernel, n_groups=n_groups, n_maps=n_maps, k_width=k_width, v_width=v_width,
                          has_cache=has_cache, diff=diff, lambda_init=lambda_init, n_chunks=n_chunks),
        grid=(n_batch, n_pos // tq),
        in_specs=in_specs,
        out_specs=pl.BlockSpec((1, tq, q_w), lambda b, i: (b, i, 0)),
        out_shape=jax.ShapeDtypeStruct((n_batch, n_pos, q_w), BF16),
        scratch_shapes=[pltpu.VMEM((n_chunks, kc_rows, k_width), BF16),
                        pltpu.VMEM((n_chunks, n_groups, v_width // n_groups + ATTN_ONES_ROWS, kc_rows), BF16),
                        pltpu.VMEM((n_groups, k_width // n_groups if diff else k_width, n_maps * tq), BF16),
                        pltpu.VMEM((n_groups, kc_rows, n_maps * tq), F32),
                        pltpu.VMEM((n_groups, kc_rows, n_maps * tq), F32)],
        compiler_params=_params("arbitrary", "arbitrary"),
        name="diff_attention" if diff else "gqa_attention",
    )(*args)
    return out.reshape(n_batch * n_pos, q_w)


def _merge_kernel(f_ref, og_ref, od_ref, gt_ref, x_ref, wf_ref, wg_ref, wd_ref, wo_ref,
                  g1_ref, n2_ref, sh2_ref, sc2_ref, xo_ref, h2_ref):
    def branch(j, a_ref, w_ref):
        gate = gt_ref[:, j * D_MODEL:(j + 1) * D_MODEL].astype(F32)
        return gate * jnp.dot(a_ref[...], w_ref[...], preferred_element_type=F32)

    merged = branch(0, f_ref, wf_ref) + branch(1, og_ref, wg_ref) + branch(2, od_ref, wd_ref)
    x = x_ref[...] + g1_ref[...] * jnp.dot(merged.astype(BF16), wo_ref[...], preferred_element_type=F32)
    xo_ref[...] = x
    h = x * lax.rsqrt(jnp.mean(x * x, axis=-1, keepdims=True) + EPS) * n2_ref[...]
    h2_ref[...] = (h * (1.0 + sc2_ref[...]) + sh2_ref[...]).astype(h2_ref.dtype)


def _merge(f, og, od, gates, x, w_f, w_go, w_do, w_out, mod, norm2_g, *, row0, rows_per_batch):
    t = x.shape[0]
    tm = TOKEN_TILE
    row = lambda width: pl.BlockSpec((tm, width), lambda i: (i, 0))
    return pl.pallas_call(
        _merge_kernel,
        grid=(t // tm,),
        in_specs=[
            row(FOURIER_WIDTH), row(GQA_WIDTH), row(DIFF_WIDTH), row(GATES_WIDTH), row(D_MODEL),
            _const_spec(w_f.shape), _const_spec(w_go.shape), _const_spec(w_do.shape), _const_spec(w_out.shape),
            _mod_spec(2, row0, rows_per_batch, tm),
            _const_spec((1, D_MODEL)),
            _mod_spec(3, row0, rows_per_batch, tm),
            _mod_spec(4, row0, rows_per_batch, tm),
        ],
        out_specs=[row(D_MODEL), row(D_MODEL)],
        out_shape=[jax.ShapeDtypeStruct((t, D_MODEL), F32), jax.ShapeDtypeStruct((t, D_MODEL), BF16)],
        compiler_params=_params("arbitrary"),
        name="merge",
    )(f, og, od, gates, x, w_f, w_go, w_do, w_out, mod, norm2_g, mod, mod)


def _routing_weights(scores, sel):
    rows = [sel[e:e + 1] for e in range(N_EXPERTS)]
    group_score = []
    for g in range(N_EXPERT_GROUPS):
        a, b, c, d = rows[EXPERTS_PER_GROUP * g:EXPERTS_PER_GROUP * (g + 1)]
        hi1, lo1, hi2, lo2 = jnp.maximum(a, b), jnp.minimum(a, b), jnp.maximum(c, d), jnp.minimum(c, d)
        group_score.append(jnp.maximum(hi1, hi2) + jnp.maximum(jnp.minimum(hi1, hi2), jnp.maximum(lo1, lo2)))
    best, best_idx = group_score[0], jnp.zeros_like(group_score[0], dtype=jnp.int32)
    for g in range(1, N_EXPERT_GROUPS):
        better = group_score[g] > best
        best = jnp.where(better, group_score[g], best)
        best_idx = jnp.where(better, g, best_idx)
    picked = []
    for e in range(N_EXPERTS):
        g = e // EXPERTS_PER_GROUP
        rank = jnp.zeros_like(best_idx)
        for j in range(EXPERTS_PER_GROUP * g, EXPERTS_PER_GROUP * (g + 1)):
            if j != e:
                ahead = (rows[j] > rows[e]) | ((rows[j] == rows[e]) & (j < e))
                rank = rank + ahead.astype(jnp.int32)
        picked.append((best_idx == g) & (rank < 2))
    weight = [jnp.where(picked[e], scores[e:e + 1], 0.0) for e in range(N_EXPERTS)]
    total = weight[0]
    for e in range(1, N_EXPERTS):
        total = total + weight[e]
    row_id = lax.broadcasted_iota(jnp.int32, scores.shape, 0)
    comb = jnp.zeros_like(scores)
    for e in range(N_EXPERTS):
        comb = jnp.where(row_id == e, weight[e] / total, comb)
    return comb, best_idx


def _route_kernel(h_ref, wr_ref, rb_ref, tri_ref, comb_ref, pos_ref, seg_ref):
    tm = h_ref.shape[0]
    logits = lax.dot_general(wr_ref[...], h_ref[...], (((1,), (1,)), ((), ())),
                             preferred_element_type=F32)
    scores = jax.nn.sigmoid(logits)
    comb_t, best_idx = _routing_weights(scores, scores + rb_ref[...])

    grp = lax.broadcasted_iota(jnp.int32, (ROUTE_ROWS, tm), 0)
    member = grp == best_idx
    prefix = jnp.dot(jnp.where(member, 1.0, 0.0).astype(BF16), tri_ref[...], preferred_element_type=F32)
    counts = prefix[:, tm - 1:tm]
    row = lax.broadcasted_iota(jnp.int32, (ROUTE_ROWS, 1), 0)
    first = jnp.zeros((ROUTE_ROWS, 1), F32)
    for g in range(1, N_EXPERT_GROUPS + 1):
        first = jnp.where(row == g, jnp.sum(jnp.where(row < g, counts, 0.0), axis=0, keepdims=True), first)
    pos = jnp.sum(jnp.where(member, first + prefix - 1.0, 0.0), axis=0, keepdims=True)

    pos_ref[0] = pos.astype(jnp.int32)
    seg_ref[0] = jnp.broadcast_to(first[:SUBLANES], (SUBLANES, LANES)).astype(jnp.int32)
    stacked = jnp.concatenate([comb_t, pos, jnp.zeros((LANES - N_EXPERTS - 1, tm), F32)], axis=0)
    comb_ref[...] = stacked.T


def _route(h2, w_router_t, router_bias):
    t = h2.shape[0]
    tm = TOKEN_TILE
    n_tiles = t // tm
    tri = jnp.asarray(np.triu(np.ones((tm, tm), np.float32)), BF16)
    comb, pos, seg = pl.pallas_call(
        _route_kernel,
        grid=(n_tiles,),
        in_specs=[pl.BlockSpec((tm, D_MODEL), lambda i: (i, 0)), _const_spec(w_router_t.shape),
                  _const_spec(router_bias.shape), _const_spec(tri.shape)],
        out_specs=[pl.BlockSpec((tm, LANES), lambda i: (i, 0)),
                   pl.BlockSpec((1, 1, tm), lambda i: (i, 0, 0)),
                   pl.BlockSpec((1, SUBLANES, LANES), lambda i: (i, 0, 0))],
        out_shape=[jax.ShapeDtypeStruct((t, LANES), F32),
                   jax.ShapeDtypeStruct((n_tiles, 1, tm), jnp.int32),
                   jax.ShapeDtypeStruct((n_tiles, SUBLANES, LANES), jnp.int32)],
        compiler_params=_params("arbitrary"),
        name="route",
    )(h2, w_router_t, router_bias, tri)
    return comb, pos, seg[:, :, 0].reshape(n_tiles * SUBLANES)


def _moe_kernel(seg_ref, h_ref, x_ref, comb_ref, pos_ref, wgu_ref, wd_ref, g2_ref, fg_ref, o_ref,
                hs_ref, cs_ref, ys_ref, *, final):
    i = pl.program_id(0)
    tm = h_ref.shape[0]
    comb = comb_ref[...]
    pos_col = comb[:, POS_LANE:POS_LANE + 1].astype(jnp.int32)
    to_sorted = jnp.where(lax.broadcasted_iota(jnp.int32, (tm, tm), 0) == pos_ref[0], 1.0, 0.0).astype(BF16)
    to_token = jnp.where(lax.broadcasted_iota(jnp.int32, (tm, tm), 1) == pos_col, 1.0, 0.0).astype(BF16)

    hs_ref[...] = jnp.dot(to_sorted, h_ref[...], preferred_element_type=F32).astype(BF16)
    hi = comb.astype(BF16)
    lo = (comb - hi.astype(F32)).astype(BF16)
    cs_ref[...] = (jnp.dot(to_sorted, hi, preferred_element_type=F32)
                   + jnp.dot(to_sorted, lo, preferred_element_type=F32))
    ys_ref[...] = jnp.zeros_like(ys_ref)

    for g in range(N_EXPERT_GROUPS):
        first = seg_ref[i * SUBLANES + g]
        last = seg_ref[i * SUBLANES + g + 1]
        shift = MOE_BLOCK.bit_length() - 1

        def block(b, carry, g=g):
            rows = pl.ds(pl.multiple_of(b * MOE_BLOCK, MOE_BLOCK), MOE_BLOCK)
            hb = hs_ref[rows, :]
            cb = cs_ref[rows, :]
            y = jnp.zeros((MOE_BLOCK, D_MODEL), F32)
            for e in range(EXPERTS_PER_GROUP * g, EXPERTS_PER_GROUP * (g + 1)):
                a = jnp.dot(hb, wgu_ref[e], preferred_element_type=F32)
                gate, up = a[:, :D_EXPERT], a[:, D_EXPERT:]
                act = gate * jax.nn.sigmoid(gate) * up * cb[:, e:e + 1]
                y = y + jnp.dot(act.astype(BF16), wd_ref[e], preferred_element_type=F32)
            ys_ref[rows, :] += y
            return carry

        lax.fori_loop(lax.shift_right_logical(first, shift),
                      lax.shift_right_logical(last + (MOE_BLOCK - 1), shift), block, 0)

    y = x_ref[...] + g2_ref[...] * jnp.dot(to_token, ys_ref[...].astype(BF16), preferred_element_type=F32)
    if final:
        y = y * lax.rsqrt(jnp.mean(y * y, axis=-1, keepdims=True) + EPS) * fg_ref[...]
    o_ref[...] = y


def _moe(h2, x, w_router_t, router_bias, w_gu, w_d, mod, final_g, *, row0, rows_per_batch, final):
    t = x.shape[0]
    tm = TOKEN_TILE
    comb, pos, seg = _route(h2, w_router_t, router_bias)
    row = lambda width: pl.BlockSpec((tm, width), lambda i, *_: (i, 0))
    return pl.pallas_call(
        functools.partial(_moe_kernel, final=final),
        grid_spec=pltpu.PrefetchScalarGridSpec(
            num_scalar_prefetch=1,
            grid=(t // tm,),
            in_specs=[
                row(D_MODEL), row(D_MODEL), row(LANES),
                pl.BlockSpec((1, 1, tm), lambda i, *_: (i, 0, 0)),
                _const_spec(w_gu.shape), _const_spec(w_d.shape),
                _mod_spec(5, row0, rows_per_batch, tm),
                _const_spec((1, D_MODEL)),
            ],
            out_specs=row(D_MODEL),
            scratch_shapes=[pltpu.VMEM((tm, D_MODEL), BF16), pltpu.VMEM((tm, LANES), F32),
                            pltpu.VMEM((tm, D_MODEL), F32)],
        ),
        out_shape=jax.ShapeDtypeStruct((t, D_MODEL), F32),
        compiler_params=_params("arbitrary"),
        name="moe",
    )(seg, h2, x, comb, pos, w_gu, w_d, mod, final_g)


def _rope_tables(n_tokens):
    rows = n_tokens // GRID_W
    row = np.repeat(np.arange(rows), GRID_W).astype(np.float64)
    col = np.tile(np.arange(GRID_W), rows).astype(np.float64)
    inv = ROPE_THETA ** (-(np.arange(AXIS_DIM // 2, dtype=np.float64) * 2.0 / AXIS_DIM))
    ang_r = row[:, None] * inv[None, :]
    ang_c = col[:, None] * inv[None, :]
    ang = np.concatenate([ang_r, ang_r, ang_c, ang_c], axis=-1)
    ang = np.concatenate([ang, ang], axis=-1)
    first_half = (np.arange(LANES) % AXIS_DIM) < AXIS_DIM // 2
    sin = np.sin(ang)
    sin_next = np.where(first_half[None, :], -sin, 0.0)
    sin_prev = np.where(first_half[None, :], 0.0, sin)
    return tuple(jnp.asarray(a, F32) for a in (np.cos(ang), sin_next, sin_prev))


def kernel(x_prompt, x_sample, cache_gqa_k, cache_gqa_v, cache_diff_k, cache_diff_v, c, c_ctx, norm1_g, w_mod,
           b_mod, w_in, gqa_q_norm, gqa_k_norm, diff_lambda, diff_subln_g, w_fourier, w_gqa_o, w_diff_o, w_out,
           norm2_g, w_router, router_bias, w_e_gate, w_e_up, w_e_down, final_norm_g):
    n_ctx_b, n_ctx, _ = x_prompt.shape
    n_lat_b, n_lat, _ = x_sample.shape
    past = cache_gqa_k.shape[2]
    assert n_lat_b + 1 <= MOD_ROWS

    w_in_b = w_in.astype(BF16)
    w_f_b, w_go_b, w_do_b, w_out_b = (w.astype(BF16) for w in (w_fourier, w_gqa_o, w_diff_o, w_out))
    w_gu_b = jnp.concatenate([w_e_gate, w_e_up], axis=-1).astype(BF16)
    w_d_b = w_e_down.astype(BF16)
    w_router_t = w_router.T.astype(BF16)
    rbias = router_bias.reshape(N_EXPERTS, 1)
    head_id = np.arange(GQA_WIDTH) // HEAD_DIM
    bd = jnp.asarray((head_id[:, None] == head_id[None, :]) / HEAD_DIM, BF16)
    final_g = final_norm_g.reshape(1, D_MODEL)

    cond = jnp.concatenate([c_ctx[None, :], c, jnp.zeros((MOD_ROWS - 1 - n_lat_b, D_MODEL), F32)], axis=0)
    mod = _adaln(cond, w_mod, b_mod).reshape(DEPTH, MOD_ROWS, 6, 1, D_MODEL)

    lat_rope = _rope_tables(n_lat)
    no_rope = tuple(jnp.zeros((TOKEN_TILE, LANES), F32) for _ in range(3))

    def run_pass(x, n_batch, n_pos, row0, rope_tabs, rope, caches, kv_dtype):
        t = n_batch * n_pos
        rows_per_batch = n_pos if row0 else t
        x = x.reshape(t, D_MODEL)
        kv_out = []
        for l in range(DEPTH):
            lambda_init = 0.8 - 0.6 * math.exp(-0.3 * l)
            uf, qg, kg, vg, qd, kd, vd, gates = _proj(
                x, norm1_g[l].reshape(1, D_MODEL), mod[l], w_in_b[l], bd,
                jnp.tile(gqa_q_norm[l], GQA_HEADS).reshape(1, GQA_WIDTH),
                jnp.tile(gqa_k_norm[l], GQA_KV_HEADS).reshape(1, GQA_KV_WIDTH),
                rope_tabs, row0=row0, rows_per_batch=rows_per_batch, rope=rope, kv_dtype=kv_dtype)
            kv_out.append((kg, vg, kd, vd))
            four = _fourier_large(uf, n_batch, n_pos) if n_pos == DFT_RADIX ** 2 else _fourier_small(uf, n_batch, n_pos)
            cg = cd = None
            if caches is not None:
                cg = (caches[0][:, l].reshape(n_batch, past, GQA_KV_WIDTH),
                      caches[1][:, l].reshape(n_batch, past, GQA_KV_WIDTH))
                cd = (caches[2][:, l].reshape(n_batch, past, DIFF_WIDTH),
                      caches[3][:, l].reshape(n_batch, past, DIFF_WIDTH))
            og = _attention(qg, kg, vg, cg, None, n_batch=n_batch, n_pos=n_pos, diff=False, lambda_init=0.0)
            od = _attention(qd, kd, vd, cd,
                            (diff_lambda[l], diff_subln_g[l].reshape(2 * HEAD_DIM, 1)),
                            n_batch=n_batch, n_pos=n_pos, diff=True, lambda_init=lambda_init)
            x, h2 = _merge(four, og, od, gates, x, w_f_b[l], w_go_b[l], w_do_b[l], w_out_b[l], mod[l],
                           norm2_g[l].reshape(1, D_MODEL), row0=row0, rows_per_batch=rows_per_batch)
            x = _moe(h2, x, w_router_t, rbias, w_gu_b[l], w_d_b[l], mod[l], final_g,
                     row0=row0, rows_per_batch=rows_per_batch, final=(l == DEPTH - 1))
        return x.reshape(n_batch, n_pos, D_MODEL), kv_out

    y_prompt, kv = run_pass(x_prompt, n_ctx_b, n_ctx, 0, no_rope, False, None, F32)
    y_sample, _ = run_pass(x_sample, n_lat_b, n_lat, 1, lat_rope, True,
                           (cache_gqa_k, cache_gqa_v, cache_diff_k, cache_diff_v), BF16)

    def stack(idx, shape):
        return jnp.stack([kv[l][idx].reshape((n_ctx_b, n_ctx) + shape) for l in range(DEPTH)], axis=1)

    return (y_prompt, y_sample,
            stack(0, (GQA_KV_HEADS, HEAD_DIM)), stack(1, (GQA_KV_HEADS, HEAD_DIM)),
            stack(2, (DIFF_HEADS, 2, HEAD_DIM)), stack(3, (DIFF_HEADS, 2 * HEAD_DIM)))
```

```python
import functools
import math

import jax
import jax.numpy as jnp
import numpy as np
from jax import lax
from jax.experimental import pallas as pl
from jax.experimental.pallas import tpu as pltpu

F32 = jnp.float32
BF16 = jnp.bfloat16

D_MODEL = 1024
DEPTH = 2
GRID_W = 64
HEAD_DIM = 64
AXIS_DIM = HEAD_DIM // 2
GQA_HEADS = 8
GQA_KV_HEADS = 2
DIFF_HEADS = 4
FOURIER_GROUPS = 4
FOURIER_GROUP_DIM = 128
FOURIER_WIDTH = FOURIER_GROUPS * FOURIER_GROUP_DIM
GQA_WIDTH = GQA_HEADS * HEAD_DIM
GQA_KV_WIDTH = GQA_KV_HEADS * HEAD_DIM
DIFF_WIDTH = DIFF_HEADS * 2 * HEAD_DIM
N_BRANCHES = 3
GATES_WIDTH = N_BRANCHES * D_MODEL
N_EXPERTS = 16
N_EXPERT_GROUPS = 4
EXPERTS_PER_GROUP = N_EXPERTS // N_EXPERT_GROUPS
D_EXPERT = 256
ROPE_THETA = 10000.0
EPS = 1e-6
QK_SCALE = HEAD_DIM ** -0.5 * math.log2(math.e)

_OFF_UF = 0
_OFF_QG = _OFF_UF + FOURIER_WIDTH
_OFF_KG = _OFF_QG + GQA_WIDTH
_OFF_VG = _OFF_KG + GQA_KV_WIDTH
_OFF_QD = _OFF_VG + GQA_KV_WIDTH
_OFF_KD = _OFF_QD + DIFF_WIDTH
_OFF_VD = _OFF_KD + DIFF_WIDTH
_OFF_GT = _OFF_VD + DIFF_WIDTH
IN_WIDTH = _OFF_GT + GATES_WIDTH

LANES = 128
SUBLANES = 8
VMEM_LIMIT_BYTES = 56 * 1024 * 1024

MOD_ROWS = 8
TOKEN_TILE = 512
ATTN_Q_TILE = 256
ATTN_K_CHUNK = 512
ATTN_SCORE_ROWS = 256
ATTN_ONES_ROWS = 16
ROUTE_ROWS = 16
POS_LANE = N_EXPERTS
MOE_BLOCK = 128
DFT_RADIX = 64
DFT_S1_COLS = 4096
DFT_S2_ROWS = 8


def _params(*sem):
    return pltpu.CompilerParams(dimension_semantics=sem, vmem_limit_bytes=VMEM_LIMIT_BYTES)


def _const_spec(shape):
    nd = len(shape)
    return pl.BlockSpec(shape, lambda *_: (0,) * nd, pipeline_mode=pl.Buffered(1))


def _adaln_kernel(c_ref, w_ref, b_ref, o_ref):
    c = c_ref[...]
    s = c * jax.nn.sigmoid(c)
    o_ref[0] = jnp.dot(s, w_ref[0], preferred_element_type=F32, precision=lax.Precision.HIGHEST) + b_ref[0]


def _adaln(cond, w_mod, b_mod):
    tn = 1536
    n = 6 * D_MODEL
    return pl.pallas_call(
        _adaln_kernel,
        grid=(DEPTH, n // tn),
        in_specs=[
            pl.BlockSpec((MOD_ROWS, D_MODEL), lambda l, j: (0, 0)),
            pl.BlockSpec((1, D_MODEL, tn), lambda l, j: (l, 0, j)),
            pl.BlockSpec((1, 1, tn), lambda l, j: (l, 0, j)),
        ],
        out_specs=pl.BlockSpec((1, MOD_ROWS, tn), lambda l, j: (l, 0, j)),
        out_shape=jax.ShapeDtypeStruct((DEPTH, MOD_ROWS, n), F32),
        compiler_params=_params("arbitrary", "arbitrary"),
        name="adaln",
    )(cond, w_mod, b_mod.reshape(DEPTH, 1, n))


def _mod_spec(chunk, row0, rows_per_batch, tm):
    return pl.BlockSpec((None, None, 1, D_MODEL),
                        lambda i, *_: (row0 + (i * tm) // rows_per_batch, chunk, 0, 0))


def _rope(x, cos, sin_next, sin_prev):
    return x * cos + pltpu.roll(x, LANES - AXIS_DIM // 2, 1) * sin_next + pltpu.roll(x, AXIS_DIM // 2, 1) * sin_prev


def _proj_kernel(x_ref, g_ref, sh_ref, sc_ref, w_ref, bd_ref, qn_ref, kn_ref, cos_ref, sn_ref, sp_ref,
                 uf_ref, qg_ref, kg_ref, vg_ref, qd_ref, kd_ref, vd_ref, gt_ref, *, rope):
    x = x_ref[...]
    h = x * lax.rsqrt(jnp.mean(x * x, axis=-1, keepdims=True) + EPS) * g_ref[...]
    hb = (h * (1.0 + sc_ref[...]) + sh_ref[...]).astype(BF16)

    def mm(lo, width):
        return jnp.dot(hb, w_ref[:, lo:lo + width], preferred_element_type=F32)

    def head_norm(z, gain):
        width = z.shape[-1]
        ms = jnp.dot((z * z).astype(BF16), bd_ref[:width, :width], preferred_element_type=F32)
        return z * lax.rsqrt(ms + EPS) * gain

    def rotary(z):
        if not rope:
            return z
        cos, sn, sp = cos_ref[...], sn_ref[...], sp_ref[...]
        return jnp.concatenate(
            [_rope(z[:, c:c + LANES], cos, sn, sp) for c in range(0, z.shape[-1], LANES)], axis=-1)

    uf_ref[...] = mm(_OFF_UF, FOURIER_WIDTH).astype(uf_ref.dtype)
    qg_ref[...] = (rotary(head_norm(mm(_OFF_QG, GQA_WIDTH), qn_ref[...])) * QK_SCALE).astype(qg_ref.dtype)
    kg_ref[...] = rotary(head_norm(mm(_OFF_KG, GQA_KV_WIDTH), kn_ref[...])).astype(kg_ref.dtype)
    vg_ref[...] = mm(_OFF_VG, GQA_KV_WIDTH).astype(vg_ref.dtype)
    qd_ref[...] = (rotary(mm(_OFF_QD, DIFF_WIDTH)) * QK_SCALE).astype(qd_ref.dtype)
    kd_ref[...] = rotary(mm(_OFF_KD, DIFF_WIDTH)).astype(kd_ref.dtype)
    vd_ref[...] = mm(_OFF_VD, DIFF_WIDTH).astype(vd_ref.dtype)
    for j in range(N_BRANCHES):
        z = mm(_OFF_GT + j * D_MODEL, D_MODEL)
        gt_ref[:, j * D_MODEL:(j + 1) * D_MODEL] = jax.nn.sigmoid(z).astype(gt_ref.dtype)


def _proj(x, norm_g, mod, w_in, bd, qn, kn, rope_tabs, *, row0, rows_per_batch, rope, kv_dtype):
    t = x.shape[0]
    tm = TOKEN_TILE
    cos, sn, sp = rope_tabs
    n_pos = cos.shape[0]
    row = lambda width: pl.BlockSpec((tm, width), lambda i: (i, 0))
    tab = pl.BlockSpec((tm, LANES), lambda i: (i % (n_pos // tm), 0))
    outs = [(FOURIER_WIDTH, BF16), (GQA_WIDTH, BF16), (GQA_KV_WIDTH, kv_dtype), (GQA_KV_WIDTH, kv_dtype),
            (DIFF_WIDTH, BF16), (DIFF_WIDTH, kv_dtype), (DIFF_WIDTH, kv_dtype), (GATES_WIDTH, BF16)]
    return pl.pallas_call(
        functools.partial(_proj_kernel, rope=rope),
        grid=(t // tm,),
        in_specs=[
            row(D_MODEL),
            _const_spec((1, D_MODEL)),
            _mod_spec(0, row0, rows_per_batch, tm),
            _mod_spec(1, row0, rows_per_batch, tm),
            _const_spec((D_MODEL, IN_WIDTH)),
            _const_spec((GQA_WIDTH, GQA_WIDTH)),
            _const_spec((1, GQA_WIDTH)),
            _const_spec((1, GQA_KV_WIDTH)),
            tab, tab, tab,
        ],
        out_specs=[row(w) for w, _ in outs],
        out_shape=[jax.ShapeDtypeStruct((t, w), dt) for w, dt in outs],
        compiler_params=_params("arbitrary"),
        name="proj",
    )(x, norm_g, mod, mod, w_in, bd, qn, kn, cos, sn, sp)


def _cos_sin(n_rows, n_cols, period, scale=1.0):
    r = np.arange(n_rows, dtype=np.int64)[:, None]
    c = np.arange(n_cols, dtype=np.int64)[None, :]
    ang = 2.0 * np.pi * ((r * c) % period).astype(np.float64) / period
    return np.cos(ang) * scale, np.sin(ang) * scale


def _table(a):
    return jnp.asarray(a, F32).astype(BF16)


def _dft_small_kernel(u_ref, cs_c_ref, cs_n_ref, o_ref):
    u = u_ref[0]
    cs_c = cs_c_ref[...]
    cs_n = cs_n_ref[...]
    for g in range(FOURIER_GROUPS):
        lo = g * FOURIER_GROUP_DIM
        t = jnp.dot(u[:, lo:lo + FOURIER_GROUP_DIM], cs_c, preferred_element_type=F32).astype(BF16)
        stacked = jnp.concatenate([t[:, :FOURIER_GROUP_DIM], t[:, FOURIER_GROUP_DIM:]], axis=0)
        o_ref[0, :, lo:lo + FOURIER_GROUP_DIM] = jnp.dot(
            cs_n, stacked, preferred_element_type=F32).astype(o_ref.dtype)


def _fourier_small(u, n_batch, n_pos):
    cc, sc = _cos_sin(FOURIER_GROUP_DIM, FOURIER_GROUP_DIM, FOURIER_GROUP_DIM,
                      scale=(n_pos * FOURIER_GROUP_DIM) ** -0.5)
    cn, sn = _cos_sin(n_pos, n_pos, n_pos)
    cs_c = _table(np.concatenate([cc, sc], axis=1))
    cs_n = _table(np.concatenate([cn, -sn], axis=1))
    blk = pl.BlockSpec((1, n_pos, FOURIER_WIDTH), lambda b: (b, 0, 0))
    out = pl.pallas_call(
        _dft_small_kernel,
        grid=(n_batch,),
        in_specs=[blk, _const_spec(cs_c.shape), _const_spec(cs_n.shape)],
        out_specs=blk,
        out_shape=jax.ShapeDtypeStruct((n_batch, n_pos, FOURIER_WIDTH), BF16),
        compiler_params=_params("arbitrary"),
        name="fourier_small",
    )(u.reshape(n_batch, n_pos, FOURIER_WIDTH), cs_c, cs_n)
    return out.reshape(n_batch * n_pos, FOURIER_WIDTH)


def _dft_stage1_kernel(u_ref, f_ref, o_ref):
    o_ref[0] = jnp.dot(f_ref[...], u_ref[0], preferred_element_type=F32).astype(o_ref.dtype)


def _dft_stage2_kernel(a_ref, m_ref, cs_ref, o_ref):
    cs = cs_ref[...]
    for i in range(DFT_S2_ROWS):
        a = jnp.concatenate([a_ref[0, 0, i], a_ref[0, 1, i]], axis=0)
        b = jnp.dot(m_ref[i], a, preferred_element_type=F32).astype(BF16)
        br, bi = b[:DFT_RADIX], b[DFT_RADIX:]
        for g in range(FOURIER_GROUPS):
            lo = g * FOURIER_GROUP_DIM
            lhs = jnp.concatenate([br[:, lo:lo + FOURIER_GROUP_DIM], bi[:, lo:lo + FOURIER_GROUP_DIM]], axis=1)
            col = i * FOURIER_WIDTH + lo
            o_ref[0, :, col:col + FOURIER_GROUP_DIM] = jnp.dot(
                lhs, cs, preferred_element_type=F32).astype(o_ref.dtype)


def _fourier_large(u, n_batch, n_pos):
    r = DFT_RADIX
    assert n_pos == r * r
    wide = r * FOURIER_WIDTH
    c1, s1 = _cos_sin(r, r, r)
    f1 = _table(np.concatenate([c1, -s1], axis=0))
    k1 = np.arange(r, dtype=np.int64)[:, None, None]
    k2 = np.arange(r, dtype=np.int64)[None, :, None]
    n2 = np.arange(r, dtype=np.int64)[None, None, :]
    ang = 2.0 * np.pi * ((n2 * (r * k2 + k1)) % n_pos).astype(np.float64) / n_pos
    mr, mi = np.cos(ang), -np.sin(ang)
    m = _table(np.concatenate([np.concatenate([mr, -mi], axis=2),
                               np.concatenate([mi, mr], axis=2)], axis=1))
    cc, sc = _cos_sin(FOURIER_GROUP_DIM, FOURIER_GROUP_DIM, FOURIER_GROUP_DIM,
                      scale=(n_pos * FOURIER_GROUP_DIM) ** -0.5)
    cs = _table(np.concatenate([cc, sc], axis=0))

    a = pl.pallas_call(
        _dft_stage1_kernel,
        grid=(n_batch, wide // DFT_S1_COLS),
        in_specs=[pl.BlockSpec((1, r, DFT_S1_COLS), lambda b, j: (b, 0, j)), _const_spec(f1.shape)],
        out_specs=pl.BlockSpec((1, 2 * r, DFT_S1_COLS), lambda b, j: (b, 0, j)),
        out_shape=jax.ShapeDtypeStruct((n_batch, 2 * r, wide), BF16),
        compiler_params=_params("arbitrary", "arbitrary"),
        name="fourier_stage1",
    )(u.reshape(n_batch, r, wide), f1)

    kb = DFT_S2_ROWS
    out = pl.pallas_call(
        _dft_stage2_kernel,
        grid=(n_batch, r // kb),
        in_specs=[
            pl.BlockSpec((1, 2, kb, r, FOURIER_WIDTH), lambda b, j: (b, 0, j, 0, 0)),
            pl.BlockSpec((kb, 2 * r, 2 * r), lambda b, j: (j, 0, 0)),
            _const_spec(cs.shape),
        ],
        out_specs=pl.BlockSpec((1, r, kb * FOURIER_WIDTH), lambda b, j: (b, 0, j)),
        out_shape=jax.ShapeDtypeStruct((n_batch, r, wide), BF16),
        compiler_params=_params("arbitrary", "arbitrary"),
        name="fourier_stage2",
    )(a.reshape(n_batch, 2, r, r, FOURIER_WIDTH), m, cs)
    return out.reshape(n_batch * n_pos, FOURIER_WIDTH)


def _attn_kernel(*refs, n_groups, n_maps, k_width, v_width, has_cache, diff, lambda_init, n_chunks):
    it = iter(refs)
    q_ref, kn_ref, vn_ref = next(it), next(it), next(it)
    kc_ref = vc_ref = lam_ref = sg_ref = None
    if has_cache:
        kc_ref, vc_ref = next(it), next(it)
    if diff:
        lam_ref, sg_ref = next(it), next(it)
    o_ref, kall_ref, vt_ref, rhs_ref, s0_ref, s1_ref, p0_ref, p1_ref = (next(it) for _ in range(8))
    kc_rows = kall_ref.shape[1]
    tq = q_ref.shape[1]
    w = n_maps * tq
    kw = k_width // n_groups if diff else k_width
    vw = v_width // n_groups

    @pl.when(pl.program_id(1) == 0)
    def _():
        ones = jnp.ones((ATTN_ONES_ROWS, kc_rows), BF16)

        def put(c, k_blk, v_blk):
            kall_ref[c] = k_blk.astype(BF16)
            v_t = v_blk.astype(F32).T.astype(BF16)
            for g in range(n_groups):
                vt_ref[c, g, :vw, :] = v_t[g * vw:(g + 1) * vw]
                vt_ref[c, g, vw:, :] = ones

        n_cached = kc_ref.shape[1] // kc_rows if has_cache else 0
        for c in range(n_chunks):
            src_k, src_v, c0 = (kc_ref, vc_ref, c) if c < n_cached else (kn_ref, vn_ref, c - n_cached)
            rows = slice(c0 * kc_rows, (c0 + 1) * kc_rows)
            put(c, src_k[0, rows, :], src_v[0, rows, :])

    qt = q_ref[0].astype(F32).T.astype(BF16)
    zeros = jnp.zeros((HEAD_DIM, tq), BF16)
    for g in range(n_groups):
        cols = []
        for m in range(n_maps):
            qrow = (g * n_maps + m) * HEAD_DIM
            parts = [zeros] * (kw // HEAD_DIM)
            parts[m if diff else g] = qt[qrow:qrow + HEAD_DIM]
            cols.append(jnp.concatenate(parts, axis=0))
        rhs_ref[g] = jnp.concatenate(cols, axis=1)

    s_refs = (s0_ref, s1_ref)
    p_refs = (p0_ref, p1_ref)

    def scores(c, slot):
        for g in range(n_groups):
            k_lo = g * kw if diff else 0
            for r in range(0, kc_rows, ATTN_SCORE_ROWS):
                s_refs[slot][g, r:r + ATTN_SCORE_ROWS] = jnp.dot(
                    kall_ref[c, r:r + ATTN_SCORE_ROWS, k_lo:k_lo + kw], rhs_ref[g], preferred_element_type=F32)

    def softmax(slot, carry):
        new = []
        for g in range(n_groups):
            m_run, _, acc = carry[g]
            s = s_refs[slot][g]
            m_new = jnp.maximum(m_run, jnp.max(s, axis=0, keepdims=True))
            p_refs[slot][g] = jnp.exp2(s - m_new).astype(BF16)
            new.append((m_new, jnp.exp2(m_run - m_new), acc))
        return tuple(new)

    def values(c, slot, carry):
        return tuple((m_run, alpha, alpha * acc + jnp.dot(vt_ref[c, g], p_refs[slot][g], preferred_element_type=F32))
                     for g, (m_run, alpha, acc) in enumerate(carry))

    def tick(t, parity, carry):
        static = isinstance(t, int)
        if not static or t < n_chunks:
            scores(t, parity)
        if not static or 2 <= t < n_chunks + 2:
            carry = values(t - 2, parity, carry)
        if not static or 1 <= t < n_chunks + 1:
            carry = softmax(1 - parity, carry)
        return carry

    carry = tuple((jnp.full((1, w), -1e30, F32), jnp.ones((1, w), F32), jnp.zeros((vw + ATTN_ONES_ROWS, w), F32))
                  for _ in range(n_groups))
    n_steady = max(n_chunks - 2, 0)
    for t in range(2):
        carry = tick(t, t % 2, carry)
    carry = lax.fori_loop(0, n_steady // 2, lambda j, cr: tick(2 * j + 3, 1, tick(2 * j + 2, 0, cr)), carry)
    for t in range(2 + 2 * (n_steady // 2), n_chunks + 2):
        carry = tick(t, t % 2, carry)
    fin = carry

    if diff:
        lp = lam_ref[...]
        lam = (jnp.exp(jnp.sum(lp[0:1] * lp[1:2], axis=-1, keepdims=True))
               - jnp.exp(jnp.sum(lp[2:3] * lp[3:4], axis=-1, keepdims=True)) + lambda_init)
    outs = []
    for g in range(n_groups):
        acc = fin[g][2]
        o = acc[:vw] / acc[vw:vw + 1]
        if diff:
            o = o[:, :tq] - lam * o[:, tq:]
            o = o * lax.rsqrt(jnp.mean(o * o, axis=0, keepdims=True) + EPS) * sg_ref[...] * (1.0 - lambda_init)
            outs.append(o)
        else:
            outs.extend(o[:, m * tq:(m + 1) * tq] for m in range(n_maps))
    o_ref[0] = jnp.concatenate(outs, axis=0).T.astype(o_ref.dtype)


def _attention(q, k_new, v_new, cache, extra, *, n_batch, n_pos, diff, lambda_init):
    k_width = k_new.shape[-1]
    v_width = v_new.shape[-1]
    tq = min(ATTN_Q_TILE, n_pos)
    kc_rows = min(ATTN_K_CHUNK, n_pos)
    has_cache = cache is not None
    n_keys = n_pos + (cache[0].shape[1] if has_cache else 0)
    if has_cache:
        assert cache[0].shape[1] % kc_rows == 0
    n_chunks = n_keys // kc_rows
    n_groups = DIFF_HEADS if diff else GQA_KV_HEADS
    n_maps = 2 if diff else GQA_HEADS // GQA_KV_HEADS
    q_w = q.shape[-1]

    full = lambda width: pl.BlockSpec((1, n_pos, width), lambda b, i: (b, 0, 0))
    in_specs = [pl.BlockSpec((1, tq, q_w), lambda b, i: (b, i, 0)), full(k_width), full(v_width)]
    args = [q.reshape(n_batch, n_pos, q_w), k_new.reshape(n_batch, n_pos, k_width),
            v_new.reshape(n_batch, n_pos, v_width)]
    if has_cache:
        past = cache[0].shape[1]
        in_specs += [pl.BlockSpec((1, past, k_width), lambda b, i: (b, 0, 0)),
                     pl.BlockSpec((1, past, v_width), lambda b, i: (b, 0, 0))]
        args += list(cache)
    if diff:
        in_specs += [_const_spec(extra[0].shape), _const_spec(extra[1].shape)]
        args += list(extra)
    out = pl.pallas_call(
        functools.partial(_attn_kernel, n_groups=n_groups, n_maps=n_maps, k_width=k_width, v_width=v_width,
                          has_cache=has_cache, diff=diff, lambda_init=lambda_init, n_chunks=n_chunks),
        grid=(n_batch, n_pos // tq),
        in_specs=in_specs,
        out_specs=pl.BlockSpec((1, tq, q_w), lambda b, i: (b, i, 0)),
        out_shape=jax.ShapeDtypeStruct((n_batch, n_pos, q_w), BF16),
        scratch_shapes=[pltpu.VMEM((n_chunks, kc_rows, k_width), BF16),
                        pltpu.VMEM((n_chunks, n_groups, v_width // n_groups + ATTN_ONES_ROWS, kc_rows), BF16),
                        pltpu.VMEM((n_groups, k_width // n_groups if diff else k_width, n_maps * tq), BF16),
                        pltpu.VMEM((n_groups, kc_rows, n_maps * tq), F32),
                        pltpu.VMEM((n_groups, kc_rows, n_maps * tq), F32),
                        pltpu.VMEM((n_groups, kc_rows, n_maps * tq), BF16),
                        pltpu.VMEM((n_groups, kc_rows, n_maps * tq), BF16)],
        compiler_params=_params("arbitrary", "arbitrary"),
        name="diff_attention" if diff else "gqa_attention",
    )(*args)
    return out.reshape(n_batch * n_pos, q_w)


def _merge_kernel(f_ref, og_ref, od_ref, gt_ref, x_ref, wf_ref, wg_ref, wd_ref, wo_ref,
                  g1_ref, n2_ref, sh2_ref, sc2_ref, xo_ref, h2_ref):
    def branch(j, a_ref, w_ref):
        gate = gt_ref[:, j * D_MODEL:(j + 1) * D_MODEL].astype(F32)
        return gate * jnp.dot(a_ref[...], w_ref[...], preferred_element_type=F32)

    merged = branch(0, f_ref, wf_ref) + branch(1, og_ref, wg_ref) + branch(2, od_ref, wd_ref)
    x = x_ref[...] + g1_ref[...] * jnp.dot(merged.astype(BF16), wo_ref[...], preferred_element_type=F32)
    xo_ref[...] = x
    h = x * lax.rsqrt(jnp.mean(x * x, axis=-1, keepdims=True) + EPS) * n2_ref[...]
    h2_ref[...] = (h * (1.0 + sc2_ref[...]) + sh2_ref[...]).astype(h2_ref.dtype)


def _merge(f, og, od, gates, x, w_f, w_go, w_do, w_out, mod, norm2_g, *, row0, rows_per_batch):
    t = x.shape[0]
    tm = TOKEN_TILE
    row = lambda width: pl.BlockSpec((tm, width), lambda i: (i, 0))
    return pl.pallas_call(
        _merge_kernel,
        grid=(t // tm,),
        in_specs=[
            row(FOURIER_WIDTH), row(GQA_WIDTH), row(DIFF_WIDTH), row(GATES_WIDTH), row(D_MODEL),
            _const_spec(w_f.shape), _const_spec(w_go.shape), _const_spec(w_do.shape), _const_spec(w_out.shape),
            _mod_spec(2, row0, rows_per_batch, tm),
            _const_spec((1, D_MODEL)),
            _mod_spec(3, row0, rows_per_batch, tm),
            _mod_spec(4, row0, rows_per_batch, tm),
        ],
        out_specs=[row(D_MODEL), row(D_MODEL)],
        out_shape=[jax.ShapeDtypeStruct((t, D_MODEL), F32), jax.ShapeDtypeStruct((t, D_MODEL), BF16)],
        compiler_params=_params("arbitrary"),
        name="merge",
    )(f, og, od, gates, x, w_f, w_go, w_do, w_out, mod, norm2_g, mod, mod)


def _routing_weights(scores, sel):
    rows = [sel[e:e + 1] for e in range(N_EXPERTS)]
    group_score = []
    for g in range(N_EXPERT_GROUPS):
        a, b, c, d = rows[EXPERTS_PER_GROUP * g:EXPERTS_PER_GROUP * (g + 1)]
        hi1, lo1, hi2, lo2 = jnp.maximum(a, b), jnp.minimum(a, b), jnp.maximum(c, d), jnp.minimum(c, d)
        group_score.append(jnp.maximum(hi1, hi2) + jnp.maximum(jnp.minimum(hi1, hi2), jnp.maximum(lo1, lo2)))
    best, best_idx = group_score[0], jnp.zeros_like(group_score[0], dtype=jnp.int32)
    for g in range(1, N_EXPERT_GROUPS):
        better = group_score[g] > best
        best = jnp.where(better, group_score[g], best)
        best_idx = jnp.where(better, g, best_idx)
    picked = []
    for e in range(N_EXPERTS):
        g = e // EXPERTS_PER_GROUP
        rank = jnp.zeros_like(best_idx)
        for j in range(EXPERTS_PER_GROUP * g, EXPERTS_PER_GROUP * (g + 1)):
            if j != e:
                ahead = (rows[j] > rows[e]) | ((rows[j] == rows[e]) & (j < e))
                rank = rank + ahead.astype(jnp.int32)
        picked.append((best_idx == g) & (rank < 2))
    weight = [jnp.where(picked[e], scores[e:e + 1], 0.0) for e in range(N_EXPERTS)]
    total = weight[0]
    for e in range(1, N_EXPERTS):
        total = total + weight[e]
    row_id = lax.broadcasted_iota(jnp.int32, scores.shape, 0)
    comb = jnp.zeros_like(scores)
    for e in range(N_EXPERTS):
        comb = jnp.where(row_id == e, weight[e] / total, comb)
    return comb, best_idx


def _route_kernel(h_ref, wr_ref, rb_ref, tri_ref, comb_ref, pos_ref, seg_ref):
    tm = h_ref.shape[0]
    logits = lax.dot_general(wr_ref[...], h_ref[...], (((1,), (1,)), ((), ())),
                             preferred_element_type=F32)
    scores = jax.nn.sigmoid(logits)
    comb_t, best_idx = _routing_weights(scores, scores + rb_ref[...])

    grp = lax.broadcasted_iota(jnp.int32, (ROUTE_ROWS, tm), 0)
    member = grp == best_idx
    prefix = jnp.dot(jnp.where(member, 1.0, 0.0).astype(BF16), tri_ref[...], preferred_element_type=F32)
    counts = prefix[:, tm - 1:tm]
    row = lax.broadcasted_iota(jnp.int32, (ROUTE_ROWS, 1), 0)
    first = jnp.zeros((ROUTE_ROWS, 1), F32)
    for g in range(1, N_EXPERT_GROUPS + 1):
        first = jnp.where(row == g, jnp.sum(jnp.where(row < g, counts, 0.0), axis=0, keepdims=True), first)
    pos = jnp.sum(jnp.where(member, first + prefix - 1.0, 0.0), axis=0, keepdims=True)

    pos_ref[0] = pos.astype(jnp.int32)
    seg_ref[0] = jnp.broadcast_to(first[:SUBLANES], (SUBLANES, LANES)).astype(jnp.int32)
    stacked = jnp.concatenate([comb_t, pos, jnp.zeros((LANES - N_EXPERTS - 1, tm), F32)], axis=0)
    comb_ref[...] = stacked.T


def _route(h2, w_router_t, router_bias):
    t = h2.shape[0]
    tm = TOKEN_TILE
    n_tiles = t // tm
    tri = jnp.asarray(np.triu(np.ones((tm, tm), np.float32)), BF16)
    comb, pos, seg = pl.pallas_call(
        _route_kernel,
        grid=(n_tiles,),
        in_specs=[pl.BlockSpec((tm, D_MODEL), lambda i: (i, 0)), _const_spec(w_router_t.shape),
                  _const_spec(router_bias.shape), _const_spec(tri.shape)],
        out_specs=[pl.BlockSpec((tm, LANES), lambda i: (i, 0)),
                   pl.BlockSpec((1, 1, tm), lambda i: (i, 0, 0)),
                   pl.BlockSpec((1, SUBLANES, LANES), lambda i: (i, 0, 0))],
        out_shape=[jax.ShapeDtypeStruct((t, LANES), F32),
                   jax.ShapeDtypeStruct((n_tiles, 1, tm), jnp.int32),
                   jax.ShapeDtypeStruct((n_tiles, SUBLANES, LANES), jnp.int32)],
        compiler_params=_params("arbitrary"),
        name="route",
    )(h2, w_router_t, router_bias, tri)
    return comb, pos, seg[:, :, 0].reshape(n_tiles * SUBLANES)


def _moe_kernel(seg_ref, h_ref, x_ref, comb_ref, pos_ref, wgu_ref, wd_ref, g2_ref, fg_ref, o_ref,
                hs_ref, cs_ref, ys_ref, *, final):
    i = pl.program_id(0)
    tm = h_ref.shape[0]
    comb = comb_ref[...]
    pos_col = comb[:, POS_LANE:POS_LANE + 1].astype(jnp.int32)
    to_sorted = jnp.where(lax.broadcasted_iota(jnp.int32, (tm, tm), 0) == pos_ref[0], 1.0, 0.0).astype(BF16)
    to_token = jnp.where(lax.broadcasted_iota(jnp.int32, (tm, tm), 1) == pos_col, 1.0, 0.0).astype(BF16)

    hs_ref[...] = jnp.dot(to_sorted, h_ref[...], preferred_element_type=F32).astype(BF16)
    hi = comb.astype(BF16)
    lo = (comb - hi.astype(F32)).astype(BF16)
    cs_ref[...] = (jnp.dot(to_sorted, hi, preferred_element_type=F32)
                   + jnp.dot(to_sorted, lo, preferred_element_type=F32))
    ys_ref[...] = jnp.zeros_like(ys_ref)

    for g in range(N_EXPERT_GROUPS):
        first = seg_ref[i * SUBLANES + g]
        last = seg_ref[i * SUBLANES + g + 1]
        shift = MOE_BLOCK.bit_length() - 1

        def block(b, carry, g=g):
            rows = pl.ds(pl.multiple_of(b * MOE_BLOCK, MOE_BLOCK), MOE_BLOCK)
            hb = hs_ref[rows, :]
            cb = cs_ref[rows, :]
            y = jnp.zeros((MOE_BLOCK, D_MODEL), F32)
            for e in range(EXPERTS_PER_GROUP * g, EXPERTS_PER_GROUP * (g + 1)):
                a = jnp.dot(hb, wgu_ref[e], preferred_element_type=F32)
                gate, up = a[:, :D_EXPERT], a[:, D_EXPERT:]
                act = gate * jax.nn.sigmoid(gate) * up * cb[:, e:e + 1]
                y = y + jnp.dot(act.astype(BF16), wd_ref[e], preferred_element_type=F32)
            ys_ref[rows, :] += y
            return carry

        lax.fori_loop(lax.shift_right_logical(first, shift),
                      lax.shift_right_logical(last + (MOE_BLOCK - 1), shift), block, 0)

    y = x_ref[...] + g2_ref[...] * jnp.dot(to_token, ys_ref[...].astype(BF16), preferred_element_type=F32)
    if final:
        y = y * lax.rsqrt(jnp.mean(y * y, axis=-1, keepdims=True) + EPS) * fg_ref[...]
    o_ref[...] = y


def _moe(h2, x, w_router_t, router_bias, w_gu, w_d, mod, final_g, *, row0, rows_per_batch, final):
    t = x.shape[0]
    tm = TOKEN_TILE
    comb, pos, seg = _route(h2, w_router_t, router_bias)
    row = lambda width: pl.BlockSpec((tm, width), lambda i, *_: (i, 0))
    return pl.pallas_call(
        functools.partial(_moe_kernel, final=final),
        grid_spec=pltpu.PrefetchScalarGridSpec(
            num_scalar_prefetch=1,
            grid=(t // tm,),
            in_specs=[
                row(D_MODEL), row(D_MODEL), row(LANES),
                pl.BlockSpec((1, 1, tm), lambda i, *_: (i, 0, 0)),
                _const_spec(w_gu.shape), _const_spec(w_d.shape),
                _mod_spec(5, row0, rows_per_batch, tm),
                _const_spec((1, D_MODEL)),
            ],
            out_specs=row(D_MODEL),
            scratch_shapes=[pltpu.VMEM((tm, D_MODEL), BF16), pltpu.VMEM((tm, LANES), F32),
                            pltpu.VMEM((tm, D_MODEL), F32)],
        ),
        out_shape=jax.ShapeDtypeStruct((t, D_MODEL), F32),
        compiler_params=_params("arbitrary"),
        name="moe",
    )(seg, h2, x, comb, pos, w_gu, w_d, mod, final_g)


def _rope_tables(n_tokens):
    rows = n_tokens // GRID_W
    row = np.repeat(np.arange(rows), GRID_W).astype(np.float64)
    col = np.tile(np.arange(GRID_W), rows).astype(np.float64)
    inv = ROPE_THETA ** (-(np.arange(AXIS_DIM // 2, dtype=np.float64) * 2.0 / AXIS_DIM))
    ang_r = row[:, None] * inv[None, :]
    ang_c = col[:, None] * inv[None, :]
    ang = np.concatenate([ang_r, ang_r, ang_c, ang_c], axis=-1)
    ang = np.concatenate([ang, ang], axis=-1)
    first_half = (np.arange(LANES) % AXIS_DIM) < AXIS_DIM // 2
    sin = np.sin(ang)
    sin_next = np.where(first_half[None, :], -sin, 0.0)
    sin_prev = np.where(first_half[None, :], 0.0, sin)
    return tuple(jnp.asarray(a, F32) for a in (np.cos(ang), sin_next, sin_prev))


def kernel(x_prompt, x_sample, cache_gqa_k, cache_gqa_v, cache_diff_k, cache_diff_v, c, c_ctx, norm1_g, w_mod,
           b_mod, w_in, gqa_q_norm, gqa_k_norm, diff_lambda, diff_subln_g, w_fourier, w_gqa_o, w_diff_o, w_out,
           norm2_g, w_router, router_bias, w_e_gate, w_e_up, w_e_down, final_norm_g):
    n_ctx_b, n_ctx, _ = x_prompt.shape
    n_lat_b, n_lat, _ = x_sample.shape
    past = cache_gqa_k.shape[2]
    assert n_lat_b + 1 <= MOD_ROWS

    w_in_b = w_in.astype(BF16)
    w_f_b, w_go_b, w_do_b, w_out_b = (w.astype(BF16) for w in (w_fourier, w_gqa_o, w_diff_o, w_out))
    w_gu_b = jnp.concatenate([w_e_gate, w_e_up], axis=-1).astype(BF16)
    w_d_b = w_e_down.astype(BF16)
    w_router_t = w_router.T.astype(BF16)
    rbias = router_bias.reshape(N_EXPERTS, 1)
    head_id = np.arange(GQA_WIDTH) // HEAD_DIM
    bd = jnp.asarray((head_id[:, None] == head_id[None, :]) / HEAD_DIM, BF16)
    final_g = final_norm_g.reshape(1, D_MODEL)

    cond = jnp.concatenate([c_ctx[None, :], c, jnp.zeros((MOD_ROWS - 1 - n_lat_b, D_MODEL), F32)], axis=0)
    mod = _adaln(cond, w_mod, b_mod).reshape(DEPTH, MOD_ROWS, 6, 1, D_MODEL)

    lat_rope = _rope_tables(n_lat)
    no_rope = tuple(jnp.zeros((TOKEN_TILE, LANES), F32) for _ in range(3))

    def run_pass(x, n_batch, n_pos, row0, rope_tabs, rope, caches, kv_dtype):
        t = n_batch * n_pos
        rows_per_batch = n_pos if row0 else t
        x = x.reshape(t, D_MODEL)
        kv_out = []
        for l in range(DEPTH):
            lambda_init = 0.8 - 0.6 * math.exp(-0.3 * l)
            uf, qg, kg, vg, qd, kd, vd, gates = _proj(
                x, norm1_g[l].reshape(1, D_MODEL), mod[l], w_in_b[l], bd,
                jnp.tile(gqa_q_norm[l], GQA_HEADS).reshape(1, GQA_WIDTH),
                jnp.tile(gqa_k_norm[l], GQA_KV_HEADS).reshape(1, GQA_KV_WIDTH),
                rope_tabs, row0=row0, rows_per_batch=rows_per_batch, rope=rope, kv_dtype=kv_dtype)
            kv_out.append((kg, vg, kd, vd))
            four = _fourier_large(uf, n_batch, n_pos) if n_pos == DFT_RADIX ** 2 else _fourier_small(uf, n_batch, n_pos)
            cg = cd = None
            if caches is not None:
                cg = (caches[0][:, l].reshape(n_batch, past, GQA_KV_WIDTH),
                      caches[1][:, l].reshape(n_batch, past, GQA_KV_WIDTH))
                cd = (caches[2][:, l].reshape(n_batch, past, DIFF_WIDTH),
                      caches[3][:, l].reshape(n_batch, past, DIFF_WIDTH))
            og = _attention(qg, kg, vg, cg, None, n_batch=n_batch, n_pos=n_pos, diff=False, lambda_init=0.0)
            od = _attention(qd, kd, vd, cd,
                            (diff_lambda[l], diff_subln_g[l].reshape(2 * HEAD_DIM, 1)),
                            n_batch=n_batch, n_pos=n_pos, diff=True, lambda_init=lambda_init)
            x, h2 = _merge(four, og, od, gates, x, w_f_b[l], w_go_b[l], w_do_b[l], w_out_b[l], mod[l],
                           norm2_g[l].reshape(1, D_MODEL), row0=row0, rows_per_batch=rows_per_batch)
            x = _moe(h2, x, w_router_t, rbias, w_gu_b[l], w_d_b[l], mod[l], final_g,
                     row0=row0, rows_per_batch=rows_per_batch, final=(l == DEPTH - 1))
        return x.reshape(n_batch, n_pos, D_MODEL), kv_out

    y_prompt, kv = run_pass(x_prompt, n_ctx_b, n_ctx, 0, no_rope, False, None, F32)
    y_sample, _ = run_pass(x_sample, n_lat_b, n_lat, 1, lat_rope, True,
                           (cache_gqa_k, cache_gqa_v, cache_diff_k, cache_diff_v), BF16)

    def stack(idx, shape):
        return jnp.stack([kv[l][idx].reshape((n_ctx_b, n_ctx) + shape) for l in range(DEPTH)], axis=1)

    return (y_prompt, y_sample,
            stack(0, (GQA_KV_HEADS, HEAD_DIM)), stack(1, (GQA_KV_HEADS, HEAD_DIM)),
            stack(2, (DIFF_HEADS, 2, HEAD_DIM)), stack(3, (DIFF_HEADS, 2 * HEAD_DIM)))
```

```python
import functools
import math

import jax
import jax.numpy as jnp
import numpy as np
from jax import lax
from jax.experimental import pallas as pl
from jax.experimental.pallas import tpu as pltpu

F32 = jnp.float32
BF16 = jnp.bfloat16

D_MODEL = 1024
DEPTH = 2
GRID_W = 64
HEAD_DIM = 64
AXIS_DIM = HEAD_DIM // 2
GQA_HEADS = 8
GQA_KV_HEADS = 2
DIFF_HEADS = 4
FOURIER_GROUPS = 4
FOURIER_GROUP_DIM = 128
FOURIER_WIDTH = FOURIER_GROUPS * FOURIER_GROUP_DIM
GQA_WIDTH = GQA_HEADS * HEAD_DIM
GQA_KV_WIDTH = GQA_KV_HEADS * HEAD_DIM
DIFF_WIDTH = DIFF_HEADS * 2 * HEAD_DIM
N_BRANCHES = 3
GATES_WIDTH = N_BRANCHES * D_MODEL
N_EXPERTS = 16
N_EXPERT_GROUPS = 4
EXPERTS_PER_GROUP = N_EXPERTS // N_EXPERT_GROUPS
D_EXPERT = 256
ROPE_THETA = 10000.0
EPS = 1e-6
QK_SCALE = HEAD_DIM ** -0.5 * math.log2(math.e)

_OFF_UF = 0
_OFF_QG = _OFF_UF + FOURIER_WIDTH
_OFF_KG = _OFF_QG + GQA_WIDTH
_OFF_VG = _OFF_KG + GQA_KV_WIDTH
_OFF_QD = _OFF_VG + GQA_KV_WIDTH
_OFF_KD = _OFF_QD + DIFF_WIDTH
_OFF_VD = _OFF_KD + DIFF_WIDTH
_OFF_GT = _OFF_VD + DIFF_WIDTH
IN_WIDTH = _OFF_GT + GATES_WIDTH

LANES = 128
SUBLANES = 8
VMEM_LIMIT_BYTES = 56 * 1024 * 1024

MOD_ROWS = 8
TOKEN_TILE = 512
ATTN_Q_TILE = 256
ATTN_K_CHUNK = 512
ATTN_SCORE_ROWS = 256
ATTN_MAPS_PER_GROUP = 2
ATTN_ONES_ROWS = 16
ROUTE_ROWS = 16
POS_LANE = N_EXPERTS
MOE_BLOCK = 128
DFT_RADIX = 64
DFT_S1_COLS = 4096
DFT_S2_ROWS = 8


def _params(*sem):
    return pltpu.CompilerParams(dimension_semantics=sem, vmem_limit_bytes=VMEM_LIMIT_BYTES)


def _const_spec(shape):
    nd = len(shape)
    return pl.BlockSpec(shape, lambda *_: (0,) * nd, pipeline_mode=pl.Buffered(1))


def _adaln_kernel(c_ref, w_ref, b_ref, o_ref):
    c = c_ref[...]
    s = c * jax.nn.sigmoid(c)
    o_ref[0] = jnp.dot(s, w_ref[0], preferred_element_type=F32, precision=lax.Precision.HIGHEST) + b_ref[0]


def _adaln(cond, w_mod, b_mod):
    tn = 1536
    n = 6 * D_MODEL
    return pl.pallas_call(
        _adaln_kernel,
        grid=(DEPTH, n // tn),
        in_specs=[
            pl.BlockSpec((MOD_ROWS, D_MODEL), lambda l, j: (0, 0)),
            pl.BlockSpec((1, D_MODEL, tn), lambda l, j: (l, 0, j)),
            pl.BlockSpec((1, 1, tn), lambda l, j: (l, 0, j)),
        ],
        out_specs=pl.BlockSpec((1, MOD_ROWS, tn), lambda l, j: (l, 0, j)),
        out_shape=jax.ShapeDtypeStruct((DEPTH, MOD_ROWS, n), F32),
        compiler_params=_params("arbitrary", "arbitrary"),
        name="adaln",
    )(cond, w_mod, b_mod.reshape(DEPTH, 1, n))


def _mod_spec(chunk, row0, rows_per_batch, tm):
    return pl.BlockSpec((None, None, 1, D_MODEL),
                        lambda i, *_: (row0 + (i * tm) // rows_per_batch, chunk, 0, 0))


def _rope(x, cos, sin_next, sin_prev):
    return x * cos + pltpu.roll(x, LANES - AXIS_DIM // 2, 1) * sin_next + pltpu.roll(x, AXIS_DIM // 2, 1) * sin_prev


def _proj_kernel(x_ref, g_ref, sh_ref, sc_ref, w_ref, bd_ref, qn_ref, kn_ref, cos_ref, sn_ref, sp_ref,
                 uf_ref, qg_ref, kg_ref, vg_ref, qd_ref, kd_ref, vd_ref, gt_ref, *, rope):
    x = x_ref[...]
    h = x * lax.rsqrt(jnp.mean(x * x, axis=-1, keepdims=True) + EPS) * g_ref[...]
    hb = (h * (1.0 + sc_ref[...]) + sh_ref[...]).astype(BF16)

    def mm(lo, width):
        return jnp.dot(hb, w_ref[:, lo:lo + width], preferred_element_type=F32)

    def head_norm(z, gain):
        width = z.shape[-1]
        ms = jnp.dot((z * z).astype(BF16), bd_ref[:width, :width], preferred_element_type=F32)
        return z * lax.rsqrt(ms + EPS) * gain

    def rotary(z):
        if not rope:
            return z
        cos, sn, sp = cos_ref[...], sn_ref[...], sp_ref[...]
        return jnp.concatenate(
            [_rope(z[:, c:c + LANES], cos, sn, sp) for c in range(0, z.shape[-1], LANES)], axis=-1)

    uf_ref[...] = mm(_OFF_UF, FOURIER_WIDTH).astype(uf_ref.dtype)
    qg_ref[...] = (rotary(head_norm(mm(_OFF_QG, GQA_WIDTH), qn_ref[...])) * QK_SCALE).astype(qg_ref.dtype)
    kg_ref[...] = rotary(head_norm(mm(_OFF_KG, GQA_KV_WIDTH), kn_ref[...])).astype(kg_ref.dtype)
    vg_ref[...] = mm(_OFF_VG, GQA_KV_WIDTH).astype(vg_ref.dtype)
    qd_ref[...] = (rotary(mm(_OFF_QD, DIFF_WIDTH)) * QK_SCALE).astype(qd_ref.dtype)
    kd_ref[...] = rotary(mm(_OFF_KD, DIFF_WIDTH)).astype(kd_ref.dtype)
    vd_ref[...] = mm(_OFF_VD, DIFF_WIDTH).astype(vd_ref.dtype)
    for j in range(N_BRANCHES):
        z = mm(_OFF_GT + j * D_MODEL, D_MODEL)
        gt_ref[:, j * D_MODEL:(j + 1) * D_MODEL] = jax.nn.sigmoid(z).astype(gt_ref.dtype)


def _proj(x, norm_g, mod, w_in, bd, qn, kn, rope_tabs, *, row0, rows_per_batch, rope, kv_dtype):
    t = x.shape[0]
    tm = TOKEN_TILE
    cos, sn, sp = rope_tabs
    n_pos = cos.shape[0]
    row = lambda width: pl.BlockSpec((tm, width), lambda i: (i, 0))
    tab = pl.BlockSpec((tm, LANES), lambda i: (i % (n_pos // tm), 0))
    outs = [(FOURIER_WIDTH, BF16), (GQA_WIDTH, BF16), (GQA_KV_WIDTH, kv_dtype), (GQA_KV_WIDTH, kv_dtype),
            (DIFF_WIDTH, BF16), (DIFF_WIDTH, kv_dtype), (DIFF_WIDTH, kv_dtype), (GATES_WIDTH, BF16)]
    return pl.pallas_call(
        functools.partial(_proj_kernel, rope=rope),
        grid=(t // tm,),
        in_specs=[
            row(D_MODEL),
            _const_spec((1, D_MODEL)),
            _mod_spec(0, row0, rows_per_batch, tm),
            _mod_spec(1, row0, rows_per_batch, tm),
            _const_spec((D_MODEL, IN_WIDTH)),
            _const_spec((GQA_WIDTH, GQA_WIDTH)),
            _const_spec((1, GQA_WIDTH)),
            _const_spec((1, GQA_KV_WIDTH)),
            tab, tab, tab,
        ],
        out_specs=[row(w) for w, _ in outs],
        out_shape=[jax.ShapeDtypeStruct((t, w), dt) for w, dt in outs],
        compiler_params=_params("arbitrary"),
        name="proj",
    )(x, norm_g, mod, mod, w_in, bd, qn, kn, cos, sn, sp)


def _cos_sin(n_rows, n_cols, period, scale=1.0):
    r = np.arange(n_rows, dtype=np.int64)[:, None]
    c = np.arange(n_cols, dtype=np.int64)[None, :]
    ang = 2.0 * np.pi * ((r * c) % period).astype(np.float64) / period
    return np.cos(ang) * scale, np.sin(ang) * scale


def _table(a):
    return jnp.asarray(a, F32).astype(BF16)


def _dft_small_kernel(u_ref, cs_c_ref, cs_n_ref, o_ref):
    u = u_ref[0]
    cs_c = cs_c_ref[...]
    cs_n = cs_n_ref[...]
    for g in range(FOURIER_GROUPS):
        lo = g * FOURIER_GROUP_DIM
        t = jnp.dot(u[:, lo:lo + FOURIER_GROUP_DIM], cs_c, preferred_element_type=F32).astype(BF16)
        stacked = jnp.concatenate([t[:, :FOURIER_GROUP_DIM], t[:, FOURIER_GROUP_DIM:]], axis=0)
        o_ref[0, :, lo:lo + FOURIER_GROUP_DIM] = jnp.dot(
            cs_n, stacked, preferred_element_type=F32).astype(o_ref.dtype)


def _fourier_small(u, n_batch, n_pos):
    cc, sc = _cos_sin(FOURIER_GROUP_DIM, FOURIER_GROUP_DIM, FOURIER_GROUP_DIM,
                      scale=(n_pos * FOURIER_GROUP_DIM) ** -0.5)
    cn, sn = _cos_sin(n_pos, n_pos, n_pos)
    cs_c = _table(np.concatenate([cc, sc], axis=1))
    cs_n = _table(np.concatenate([cn, -sn], axis=1))
    blk = pl.BlockSpec((1, n_pos, FOURIER_WIDTH), lambda b: (b, 0, 0))
    out = pl.pallas_call(
        _dft_small_kernel,
        grid=(n_batch,),
        in_specs=[blk, _const_spec(cs_c.shape), _const_spec(cs_n.shape)],
        out_specs=blk,
        out_shape=jax.ShapeDtypeStruct((n_batch, n_pos, FOURIER_WIDTH), BF16),
        compiler_params=_params("arbitrary"),
        name="fourier_small",
    )(u.reshape(n_batch, n_pos, FOURIER_WIDTH), cs_c, cs_n)
    return out.reshape(n_batch * n_pos, FOURIER_WIDTH)


def _dft_stage1_kernel(u_ref, f_ref, o_ref):
    o_ref[0] = jnp.dot(f_ref[...], u_ref[0], preferred_element_type=F32).astype(o_ref.dtype)


def _dft_stage2_kernel(a_ref, m_ref, cs_ref, o_ref):
    cs = cs_ref[...]
    for i in range(DFT_S2_ROWS):
        a = jnp.concatenate([a_ref[0, 0, i], a_ref[0, 1, i]], axis=0)
        b = jnp.dot(m_ref[i], a, preferred_element_type=F32).astype(BF16)
        br, bi = b[:DFT_RADIX], b[DFT_RADIX:]
        for g in range(FOURIER_GROUPS):
            lo = g * FOURIER_GROUP_DIM
            lhs = jnp.concatenate([br[:, lo:lo + FOURIER_GROUP_DIM], bi[:, lo:lo + FOURIER_GROUP_DIM]], axis=1)
            col = i * FOURIER_WIDTH + lo
            o_ref[0, :, col:col + FOURIER_GROUP_DIM] = jnp.dot(
                lhs, cs, preferred_element_type=F32).astype(o_ref.dtype)


def _fourier_large(u, n_batch, n_pos):
    r = DFT_RADIX
    assert n_pos == r * r
    wide = r * FOURIER_WIDTH
    c1, s1 = _cos_sin(r, r, r)
    f1 = _table(np.concatenate([c1, -s1], axis=0))
    k1 = np.arange(r, dtype=np.int64)[:, None, None]
    k2 = np.arange(r, dtype=np.int64)[None, :, None]
    n2 = np.arange(r, dtype=np.int64)[None, None, :]
    ang = 2.0 * np.pi * ((n2 * (r * k2 + k1)) % n_pos).astype(np.float64) / n_pos
    mr, mi = np.cos(ang), -np.sin(ang)
    m = _table(np.concatenate([np.concatenate([mr, -mi], axis=2),
                               np.concatenate([mi, mr], axis=2)], axis=1))
    cc, sc = _cos_sin(FOURIER_GROUP_DIM, FOURIER_GROUP_DIM, FOURIER_GROUP_DIM,
                      scale=(n_pos * FOURIER_GROUP_DIM) ** -0.5)
    cs = _table(np.concatenate([cc, sc], axis=0))

    a = pl.pallas_call(
        _dft_stage1_kernel,
        grid=(n_batch, wide // DFT_S1_COLS),
        in_specs=[pl.BlockSpec((1, r, DFT_S1_COLS), lambda b, j: (b, 0, j)), _const_spec(f1.shape)],
        out_specs=pl.BlockSpec((1, 2 * r, DFT_S1_COLS), lambda b, j: (b, 0, j)),
        out_shape=jax.ShapeDtypeStruct((n_batch, 2 * r, wide), BF16),
        compiler_params=_params("arbitrary", "arbitrary"),
        name="fourier_stage1",
    )(u.reshape(n_batch, r, wide), f1)

    kb = DFT_S2_ROWS
    out = pl.pallas_call(
        _dft_stage2_kernel,
        grid=(n_batch, r // kb),
        in_specs=[
            pl.BlockSpec((1, 2, kb, r, FOURIER_WIDTH), lambda b, j: (b, 0, j, 0, 0)),
            pl.BlockSpec((kb, 2 * r, 2 * r), lambda b, j: (j, 0, 0)),
            _const_spec(cs.shape),
        ],
        out_specs=pl.BlockSpec((1, r, kb * FOURIER_WIDTH), lambda b, j: (b, 0, j)),
        out_shape=jax.ShapeDtypeStruct((n_batch, r, wide), BF16),
        compiler_params=_params("arbitrary", "arbitrary"),
        name="fourier_stage2",
    )(a.reshape(n_batch, 2, r, r, FOURIER_WIDTH), m, cs)
    return out.reshape(n_batch * n_pos, FOURIER_WIDTH)


def _attn_kernel(*refs, n_groups, n_maps, n_kv, k_width, v_width, has_cache, diff, lambda_init, n_chunks):
    it = iter(refs)
    q_ref, kn_ref, vn_ref = next(it), next(it), next(it)
    kc_ref = vc_ref = lam_ref = sg_ref = None
    if has_cache:
        kc_ref, vc_ref = next(it), next(it)
    if diff:
        lam_ref, sg_ref = next(it), next(it)
    o_ref, kall_ref, vt_ref, rhs_ref, s0_ref, s1_ref, mx0_ref, mx1_ref, p0_ref, p1_ref = (
        next(it) for _ in range(10))
    kc_rows = kall_ref.shape[1]
    tq = q_ref.shape[1]
    w = n_maps * tq
    kw = LANES
    vw = v_width // n_kv
    kv_of = lambda g: g // (n_groups // n_kv)

    def key_lane(g, m):
        return kv_of(g) * 2 * HEAD_DIM + m * HEAD_DIM if diff else kv_of(g) * HEAD_DIM

    @pl.when(pl.program_id(1) == 0)
    def _():
        ones = jnp.ones((ATTN_ONES_ROWS, kc_rows), BF16)

        def put(c, k_blk, v_blk):
            kall_ref[c] = k_blk.astype(BF16)
            v_t = v_blk.astype(F32).T.astype(BF16)
            for h in range(n_kv):
                vt_ref[c, h, :vw, :] = v_t[h * vw:(h + 1) * vw]
                vt_ref[c, h, vw:, :] = ones

        n_cached = kc_ref.shape[1] // kc_rows if has_cache else 0
        for c in range(n_chunks):
            src_k, src_v, c0 = (kc_ref, vc_ref, c) if c < n_cached else (kn_ref, vn_ref, c - n_cached)
            rows = slice(c0 * kc_rows, (c0 + 1) * kc_rows)
            put(c, src_k[0, rows, :], src_v[0, rows, :])

    qt = q_ref[0].astype(F32).T.astype(BF16)
    zeros = jnp.zeros((HEAD_DIM, tq), BF16)
    for g in range(n_groups):
        cols = []
        for m in range(n_maps):
            qrow = (g * n_maps + m) * HEAD_DIM
            parts = [zeros] * (kw // HEAD_DIM)
            parts[key_lane(g, m) % kw // HEAD_DIM] = qt[qrow:qrow + HEAD_DIM]
            cols.append(jnp.concatenate(parts, axis=0))
        rhs_ref[g] = jnp.concatenate(cols, axis=1)

    score_rows = min(kc_rows, ATTN_SCORE_ROWS)
    s_refs = (s0_ref, s1_ref)
    mx_refs = (mx0_ref, mx1_ref)
    p_refs = (p0_ref, p1_ref)

    def scores(c, slot):
        for g in range(n_groups):
            k_lo = key_lane(g, 0) // kw * kw
            mx = None
            for r in range(0, kc_rows, score_rows):
                s = jnp.dot(kall_ref[c, r:r + score_rows, k_lo:k_lo + kw], rhs_ref[g], preferred_element_type=F32)
                s_refs[slot][g, r:r + score_rows] = s
                blk_max = jnp.max(s, axis=0, keepdims=True)
                mx = blk_max if mx is None else jnp.maximum(mx, blk_max)
            mx_refs[slot][g] = mx

    def softmax(slot, carry):
        new = []
        for g in range(n_groups):
            m_run, _, acc = carry[g]
            m_new = jnp.maximum(m_run, mx_refs[slot][g])
            p_refs[slot][g] = jnp.exp2(s_refs[slot][g] - m_new).astype(BF16)
            new.append((m_new, jnp.exp2(m_run - m_new), acc))
        return tuple(new)

    def values(c, slot, carry):
        return tuple((m_run, alpha,
                      alpha * acc + jnp.dot(vt_ref[c, kv_of(g)], p_refs[slot][g], preferred_element_type=F32))
                     for g, (m_run, alpha, acc) in enumerate(carry))

    def tick(t, parity, carry):
        static = isinstance(t, int)
        if not static or t < n_chunks:
            scores(t, parity)
        if not static or 2 <= t < n_chunks + 2:
            carry = values(t - 2, parity, carry)
        if not static or 1 <= t < n_chunks + 1:
            carry = softmax(1 - parity, carry)
        return carry

    carry = tuple((jnp.full((1, w), -1e30, F32), jnp.ones((1, w), F32), jnp.zeros((vw + ATTN_ONES_ROWS, w), F32))
                  for _ in range(n_groups))
    n_steady = max(n_chunks - 2, 0)
    for t in range(2):
        carry = tick(t, t % 2, carry)
    carry = lax.fori_loop(0, n_steady // 2, lambda j, cr: tick(2 * j + 3, 1, tick(2 * j + 2, 0, cr)), carry)
    for t in range(2 + 2 * (n_steady // 2), n_chunks + 2):
        carry = tick(t, t % 2, carry)
    fin = carry

    if diff:
        lp = lam_ref[...]
        lam = (jnp.exp(jnp.sum(lp[0:1] * lp[1:2], axis=-1, keepdims=True))
               - jnp.exp(jnp.sum(lp[2:3] * lp[3:4], axis=-1, keepdims=True)) + lambda_init)
    outs = []
    for g in range(n_groups):
        acc = fin[g][2]
        o = acc[:vw] / acc[vw:vw + 1]
        if diff:
            o = o[:, :tq] - lam * o[:, tq:]
            o = o * lax.rsqrt(jnp.mean(o * o, axis=0, keepdims=True) + EPS) * sg_ref[...] * (1.0 - lambda_init)
            outs.append(o)
        else:
            outs.extend(o[:, m * tq:(m + 1) * tq] for m in range(n_maps))
    o_ref[0] = jnp.concatenate(outs, axis=0).T.astype(o_ref.dtype)


def _attention(q, k_new, v_new, cache, extra, *, n_batch, n_pos, diff, lambda_init):
    k_width = k_new.shape[-1]
    v_width = v_new.shape[-1]
    tq = min(ATTN_Q_TILE, n_pos)
    kc_rows = min(ATTN_K_CHUNK, n_pos)
    has_cache = cache is not None
    n_keys = n_pos + (cache[0].shape[1] if has_cache else 0)
    if has_cache:
        assert cache[0].shape[1] % kc_rows == 0
    n_chunks = n_keys // kc_rows
    n_kv = DIFF_HEADS if diff else GQA_KV_HEADS
    n_maps = ATTN_MAPS_PER_GROUP
    n_groups = q.shape[-1] // (n_maps * HEAD_DIM)
    q_w = q.shape[-1]

    full = lambda width: pl.BlockSpec((1, n_pos, width), lambda b, i: (b, 0, 0))
    in_specs = [pl.BlockSpec((1, tq, q_w), lambda b, i: (b, i, 0)), full(k_width), full(v_width)]
    args = [q.reshape(n_batch, n_pos, q_w), k_new.reshape(n_batch, n_pos, k_width),
            v_new.reshape(n_batch, n_pos, v_width)]
    if has_cache:
        past = cache[0].shape[1]
        in_specs += [pl.BlockSpec((1, past, k_width), lambda b, i: (b, 0, 0)),
                     pl.BlockSpec((1, past, v_width), lambda b, i: (b, 0, 0))]
        args += list(cache)
    if diff:
        in_specs += [_const_spec(extra[0].shape), _const_spec(extra[1].shape)]
        args += list(extra)
    out = pl.pallas_call(
        functools.partial(_attn_kernel, n_groups=n_groups, n_maps=n_maps, n_kv=n_kv, k_width=k_width, v_width=v_width,
                          has_cache=has_cache, diff=diff, lambda_init=lambda_init, n_chunks=n_chunks),
        grid=(n_batch, n_pos // tq),
        in_specs=in_specs,
        out_specs=pl.BlockSpec((1, tq, q_w), lambda b, i: (b, i, 0)),
        out_shape=jax.ShapeDtypeStruct((n_batch, n_pos, q_w), BF16),
        scratch_shapes=[pltpu.VMEM((n_chunks, kc_rows, k_width), BF16),
                        pltpu.VMEM((n_chunks, n_kv, v_width // n_kv + ATTN_ONES_ROWS, kc_rows), BF16),
                        pltpu.VMEM((n_groups, LANES, n_maps * tq), BF16),
                        pltpu.VMEM((n_groups, kc_rows, n_maps * tq), F32),
                        pltpu.VMEM((n_groups, kc_rows, n_maps * tq), F32),
                        pltpu.VMEM((n_groups, 1, n_maps * tq), F32),
                        pltpu.VMEM((n_groups, 1, n_maps * tq), F32),
                        pltpu.VMEM((n_groups, kc_rows, n_maps * tq), BF16),
                        pltpu.VMEM((n_groups, kc_rows, n_maps * tq), BF16)],
        compiler_params=_params("arbitrary", "arbitrary"),
        name="diff_attention" if diff else "gqa_attention",
    )(*args)
    return out.reshape(n_batch * n_pos, q_w)


def _merge_kernel(f_ref, og_ref, od_ref, gt_ref, x_ref, wf_ref, wg_ref, wd_ref, wo_ref,
                  g1_ref, n2_ref, sh2_ref, sc2_ref, xo_ref, h2_ref):
    def branch(j, a_ref, w_ref):
        gate = gt_ref[:, j * D_MODEL:(j + 1) * D_MODEL].astype(F32)
        return gate * jnp.dot(a_ref[...], w_ref[...], preferred_element_type=F32)

    merged = branch(0, f_ref, wf_ref) + branch(1, og_ref, wg_ref) + branch(2, od_ref, wd_ref)
    x = x_ref[...] + g1_ref[...] * jnp.dot(merged.astype(BF16), wo_ref[...], preferred_element_type=F32)
    xo_ref[...] = x
    h = x * lax.rsqrt(jnp.mean(x * x, axis=-1, keepdims=True) + EPS) * n2_ref[...]
    h2_ref[...] = (h * (1.0 + sc2_ref[...]) + sh2_ref[...]).astype(h2_ref.dtype)


def _merge(f, og, od, gates, x, w_f, w_go, w_do, w_out, mod, norm2_g, *, row0, rows_per_batch):
    t = x.shape[0]
    tm = TOKEN_TILE
    row = lambda width: pl.BlockSpec((tm, width), lambda i: (i, 0))
    return pl.pallas_call(
        _merge_kernel,
        grid=(t // tm,),
        in_specs=[
            row(FOURIER_WIDTH), row(GQA_WIDTH), row(DIFF_WIDTH), row(GATES_WIDTH), row(D_MODEL),
            _const_spec(w_f.shape), _const_spec(w_go.shape), _const_spec(w_do.shape), _const_spec(w_out.shape),
            _mod_spec(2, row0, rows_per_batch, tm),
            _const_spec((1, D_MODEL)),
            _mod_spec(3, row0, rows_per_batch, tm),
            _mod_spec(4, row0, rows_per_batch, tm),
        ],
        out_specs=[row(D_MODEL), row(D_MODEL)],
        out_shape=[jax.ShapeDtypeStruct((t, D_MODEL), F32), jax.ShapeDtypeStruct((t, D_MODEL), BF16)],
        compiler_params=_params("arbitrary"),
        name="merge",
    )(f, og, od, gates, x, w_f, w_go, w_do, w_out, mod, norm2_g, mod, mod)


def _routing_weights(scores, sel):
    rows = [sel[e:e + 1] for e in range(N_EXPERTS)]
    group_score = []
    for g in range(N_EXPERT_GROUPS):
        a, b, c, d = rows[EXPERTS_PER_GROUP * g:EXPERTS_PER_GROUP * (g + 1)]
        hi1, lo1, hi2, lo2 = jnp.maximum(a, b), jnp.minimum(a, b), jnp.maximum(c, d), jnp.minimum(c, d)
        group_score.append(jnp.maximum(hi1, hi2) + jnp.maximum(jnp.minimum(hi1, hi2), jnp.maximum(lo1, lo2)))
    best, best_idx = group_score[0], jnp.zeros_like(group_score[0], dtype=jnp.int32)
    for g in range(1, N_EXPERT_GROUPS):
        better = group_score[g] > best
        best = jnp.where(better, group_score[g], best)
        best_idx = jnp.where(better, g, best_idx)
    picked = []
    for e in range(N_EXPERTS):
        g = e // EXPERTS_PER_GROUP
        rank = jnp.zeros_like(best_idx)
        for j in range(EXPERTS_PER_GROUP * g, EXPERTS_PER_GROUP * (g + 1)):
            if j != e:
                ahead = (rows[j] > rows[e]) | ((rows[j] == rows[e]) & (j < e))
                rank = rank + ahead.astype(jnp.int32)
        picked.append((best_idx == g) & (rank < 2))
    weight = [jnp.where(picked[e], scores[e:e + 1], 0.0) for e in range(N_EXPERTS)]
    total = weight[0]
    for e in range(1, N_EXPERTS):
        total = total + weight[e]
    row_id = lax.broadcasted_iota(jnp.int32, scores.shape, 0)
    comb = jnp.zeros_like(scores)
    for e in range(N_EXPERTS):
        comb = jnp.where(row_id == e, weight[e] / total, comb)
    return comb, best_idx


def _route_kernel(h_ref, wr_ref, rb_ref, tri_ref, comb_ref, pos_ref, seg_ref):
    tm = h_ref.shape[0]
    logits = lax.dot_general(wr_ref[...], h_ref[...], (((1,), (1,)), ((), ())),
                             preferred_element_type=F32)
    scores = jax.nn.sigmoid(logits)
    comb_t, best_idx = _routing_weights(scores, scores + rb_ref[...])

    grp = lax.broadcasted_iota(jnp.int32, (ROUTE_ROWS, tm), 0)
    member = grp == best_idx
    prefix = jnp.dot(jnp.where(member, 1.0, 0.0).astype(BF16), tri_ref[...], preferred_element_type=F32)
    counts = prefix[:, tm - 1:tm]
    row = lax.broadcasted_iota(jnp.int32, (ROUTE_ROWS, 1), 0)
    first = jnp.zeros((ROUTE_ROWS, 1), F32)
    for g in range(1, N_EXPERT_GROUPS + 1):
        first = jnp.where(row == g, jnp.sum(jnp.where(row < g, counts, 0.0), axis=0, keepdims=True), first)
    pos = jnp.sum(jnp.where(member, first + prefix - 1.0, 0.0), axis=0, keepdims=True)

    pos_ref[0] = pos.astype(jnp.int32)
    seg_ref[0] = jnp.broadcast_to(first[:SUBLANES], (SUBLANES, LANES)).astype(jnp.int32)
    stacked = jnp.concatenate([comb_t, pos, jnp.zeros((LANES - N_EXPERTS - 1, tm), F32)], axis=0)
    comb_ref[...] = stacked.T


def _route(h2, w_router_t, router_bias):
    t = h2.shape[0]
    tm = TOKEN_TILE
    n_tiles = t // tm
    tri = jnp.asarray(np.triu(np.ones((tm, tm), np.float32)), BF16)
    comb, pos, seg = pl.pallas_call(
        _route_kernel,
        grid=(n_tiles,),
        in_specs=[pl.BlockSpec((tm, D_MODEL), lambda i: (i, 0)), _const_spec(w_router_t.shape),
                  _const_spec(router_bias.shape), _const_spec(tri.shape)],
        out_specs=[pl.BlockSpec((tm, LANES), lambda i: (i, 0)),
                   pl.BlockSpec((1, 1, tm), lambda i: (i, 0, 0)),
                   pl.BlockSpec((1, SUBLANES, LANES), lambda i: (i, 0, 0))],
        out_shape=[jax.ShapeDtypeStruct((t, LANES), F32),
                   jax.ShapeDtypeStruct((n_tiles, 1, tm), jnp.int32),
                   jax.ShapeDtypeStruct((n_tiles, SUBLANES, LANES), jnp.int32)],
        compiler_params=_params("arbitrary"),
        name="route",
    )(h2, w_router_t, router_bias, tri)
    return comb, pos, seg[:, :, 0].reshape(n_tiles * SUBLANES)


def _moe_kernel(seg_ref, h_ref, x_ref, comb_ref, pos_ref, wgu_ref, wd_ref, g2_ref, fg_ref, o_ref,
                hs_ref, cs_ref, ys_ref, *, final):
    i = pl.program_id(0)
    tm = h_ref.shape[0]
    comb = comb_ref[...]
    pos_col = comb[:, POS_LANE:POS_LANE + 1].astype(jnp.int32)
    to_sorted = jnp.where(lax.broadcasted_iota(jnp.int32, (tm, tm), 0) == pos_ref[0], 1.0, 0.0).astype(BF16)
    to_token = jnp.where(lax.broadcasted_iota(jnp.int32, (tm, tm), 1) == pos_col, 1.0, 0.0).astype(BF16)

    hs_ref[...] = jnp.dot(to_sorted, h_ref[...], preferred_element_type=F32).astype(BF16)
    hi = comb.astype(BF16)
    lo = (comb - hi.astype(F32)).astype(BF16)
    cs_ref[...] = (jnp.dot(to_sorted, hi, preferred_element_type=F32)
                   + jnp.dot(to_sorted, lo, preferred_element_type=F32))
    ys_ref[...] = jnp.zeros_like(ys_ref)

    for g in range(N_EXPERT_GROUPS):
        first = seg_ref[i * SUBLANES + g]
        last = seg_ref[i * SUBLANES + g + 1]
        shift = MOE_BLOCK.bit_length() - 1
        b0 = lax.shift_right_logical(first, shift)
        n_blocks = lax.shift_right_logical(last + (MOE_BLOCK - 1), shift) - b0
        n_pairs = lax.shift_right_logical(n_blocks, 1)

        def run(b, n_rows, g=g):
            rows = pl.ds(pl.multiple_of(b * MOE_BLOCK, MOE_BLOCK), n_rows)
            hb = hs_ref[rows, :]
            cb = cs_ref[rows, :]
            y = jnp.zeros((n_rows, D_MODEL), F32)
            for e in range(EXPERTS_PER_GROUP * g, EXPERTS_PER_GROUP * (g + 1)):
                a = jnp.dot(hb, wgu_ref[e], preferred_element_type=F32)
                gate, up = a[:, :D_EXPERT], a[:, D_EXPERT:]
                act = gate * jax.nn.sigmoid(gate) * up * cb[:, e:e + 1]
                y = y + jnp.dot(act.astype(BF16), wd_ref[e], preferred_element_type=F32)
            ys_ref[rows, :] += y

        def pair(j, carry, run=run, b0=b0):
            run(b0 + 2 * j, 2 * MOE_BLOCK)
            return carry

        lax.fori_loop(0, n_pairs, pair, 0)
        pl.when(n_blocks - 2 * n_pairs == 1)(functools.partial(run, b0 + 2 * n_pairs, MOE_BLOCK))

    y = x_ref[...] + g2_ref[...] * jnp.dot(to_token, ys_ref[...].astype(BF16), preferred_element_type=F32)
    if final:
        y = y * lax.rsqrt(jnp.mean(y * y, axis=-1, keepdims=True) + EPS) * fg_ref[...]
    o_ref[...] = y


def _moe(h2, x, w_router_t, router_bias, w_gu, w_d, mod, final_g, *, row0, rows_per_batch, final):
    t = x.shape[0]
    tm = TOKEN_TILE
    comb, pos, seg = _route(h2, w_router_t, router_bias)
    row = lambda width: pl.BlockSpec((tm, width), lambda i, *_: (i, 0))
    return pl.pallas_call(
        functools.partial(_moe_kernel, final=final),
        grid_spec=pltpu.PrefetchScalarGridSpec(
            num_scalar_prefetch=1,
            grid=(t // tm,),
            in_specs=[
                row(D_MODEL), row(D_MODEL), row(LANES),
                pl.BlockSpec((1, 1, tm), lambda i, *_: (i, 0, 0)),
                _const_spec(w_gu.shape), _const_spec(w_d.shape),
                _mod_spec(5, row0, rows_per_batch, tm),
                _const_spec((1, D_MODEL)),
            ],
            out_specs=row(D_MODEL),
            scratch_shapes=[pltpu.VMEM((tm, D_MODEL), BF16), pltpu.VMEM((tm, LANES), F32),
                            pltpu.VMEM((tm, D_MODEL), F32)],
        ),
        out_shape=jax.ShapeDtypeStruct((t, D_MODEL), F32),
        compiler_params=_params("arbitrary"),
        name="moe",
    )(seg, h2, x, comb, pos, w_gu, w_d, mod, final_g)


def _rope_tables(n_tokens):
    rows = n_tokens // GRID_W
    row = np.repeat(np.arange(rows), GRID_W).astype(np.float64)
    col = np.tile(np.arange(GRID_W), rows).astype(np.float64)
    inv = ROPE_THETA ** (-(np.arange(AXIS_DIM // 2, dtype=np.float64) * 2.0 / AXIS_DIM))
    ang_r = row[:, None] * inv[None, :]
    ang_c = col[:, None] * inv[None, :]
    ang = np.concatenate([ang_r, ang_r, ang_c, ang_c], axis=-1)
    ang = np.concatenate([ang, ang], axis=-1)
    first_half = (np.arange(LANES) % AXIS_DIM) < AXIS_DIM // 2
    sin = np.sin(ang)
    sin_next = np.where(first_half[None, :], -sin, 0.0)
    sin_prev = np.where(first_half[None, :], 0.0, sin)
    return tuple(jnp.asarray(a, F32) for a in (np.cos(ang), sin_next, sin_prev))


def kernel(x_prompt, x_sample, cache_gqa_k, cache_gqa_v, cache_diff_k, cache_diff_v, c, c_ctx, norm1_g, w_mod,
           b_mod, w_in, gqa_q_norm, gqa_k_norm, diff_lambda, diff_subln_g, w_fourier, w_gqa_o, w_diff_o, w_out,
           norm2_g, w_router, router_bias, w_e_gate, w_e_up, w_e_down, final_norm_g):
    n_ctx_b, n_ctx, _ = x_prompt.shape
    n_lat_b, n_lat, _ = x_sample.shape
    past = cache_gqa_k.shape[2]
    assert n_lat_b + 1 <= MOD_ROWS

    w_in_b = w_in.astype(BF16)
    w_f_b, w_go_b, w_do_b, w_out_b = (w.astype(BF16) for w in (w_fourier, w_gqa_o, w_diff_o, w_out))
    w_gu_b = jnp.concatenate([w_e_gate, w_e_up], axis=-1).astype(BF16)
    w_d_b = w_e_down.astype(BF16)
    w_router_t = w_router.T.astype(BF16)
    rbias = router_bias.reshape(N_EXPERTS, 1)
    head_id = np.arange(GQA_WIDTH) // HEAD_DIM
    bd = jnp.asarray((head_id[:, None] == head_id[None, :]) / HEAD_DIM, BF16)
    final_g = final_norm_g.reshape(1, D_MODEL)

    cond = jnp.concatenate([c_ctx[None, :], c, jnp.zeros((MOD_ROWS - 1 - n_lat_b, D_MODEL), F32)], axis=0)
    mod = _adaln(cond, w_mod, b_mod).reshape(DEPTH, MOD_ROWS, 6, 1, D_MODEL)

    lat_rope = _rope_tables(n_lat)
    no_rope = tuple(jnp.zeros((TOKEN_TILE, LANES), F32) for _ in range(3))

    def run_pass(x, n_batch, n_pos, row0, rope_tabs, rope, caches, kv_dtype):
        t = n_batch * n_pos
        rows_per_batch = n_pos if row0 else t
        x = x.reshape(t, D_MODEL)
        kv_out = []
        for l in range(DEPTH):
            lambda_init = 0.8 - 0.6 * math.exp(-0.3 * l)
            uf, qg, kg, vg, qd, kd, vd, gates = _proj(
                x, norm1_g[l].reshape(1, D_MODEL), mod[l], w_in_b[l], bd,
                jnp.tile(gqa_q_norm[l], GQA_HEADS).reshape(1, GQA_WIDTH),
                jnp.tile(gqa_k_norm[l], GQA_KV_HEADS).reshape(1, GQA_KV_WIDTH),
                rope_tabs, row0=row0, rows_per_batch=rows_per_batch, rope=rope, kv_dtype=kv_dtype)
            kv_out.append((kg, vg, kd, vd))
            four = _fourier_large(uf, n_batch, n_pos) if n_pos == DFT_RADIX ** 2 else _fourier_small(uf, n_batch, n_pos)
            cg = cd = None
            if caches is not None:
                cg = (caches[0][:, l].reshape(n_batch, past, GQA_KV_WIDTH),
                      caches[1][:, l].reshape(n_batch, past, GQA_KV_WIDTH))
                cd = (caches[2][:, l].reshape(n_batch, past, DIFF_WIDTH),
                      caches[3][:, l].reshape(n_batch, past, DIFF_WIDTH))
            og = _attention(qg, kg, vg, cg, None, n_batch=n_batch, n_pos=n_pos, diff=False, lambda_init=0.0)
            od = _attention(qd, kd, vd, cd,
                            (diff_lambda[l], diff_subln_g[l].reshape(2 * HEAD_DIM, 1)),
                            n_batch=n_batch, n_pos=n_pos, diff=True, lambda_init=lambda_init)
            x, h2 = _merge(four, og, od, gates, x, w_f_b[l], w_go_b[l], w_do_b[l], w_out_b[l], mod[l],
                           norm2_g[l].reshape(1, D_MODEL), row0=row0, rows_per_batch=rows_per_batch)
            x = _moe(h2, x, w_router_t, rbias, w_gu_b[l], w_d_b[l], mod[l], final_g,
                     row0=row0, rows_per_batch=rows_per_batch, final=(l == DEPTH - 1))
        return x.reshape(n_batch, n_pos, D_MODEL), kv_out

    y_prompt, kv = run_pass(x_prompt, n_ctx_b, n_ctx, 0, no_rope, False, None, F32)
    y_sample, _ = run_pass(x_sample, n_lat_b, n_lat, 1, lat_rope, True,
                           (cache_gqa_k, cache_gqa_v, cache_diff_k, cache_diff_v), BF16)

    def stack(idx, shape):
        return jnp.stack([kv[l][idx].reshape((n_ctx_b, n_ctx) + shape) for l in range(DEPTH)], axis=1)

    return (y_prompt, y_sample,
            stack(0, (GQA_KV_HEADS, HEAD_DIM)), stack(1, (GQA_KV_HEADS, HEAD_DIM)),
            stack(2, (DIFF_HEADS, 2, HEAD_DIM)), stack(3, (DIFF_HEADS, 2 * HEAD_DIM)))
```

```python
import functools
import math

import jax
import jax.numpy as jnp
import numpy as np
from jax import lax
from jax.experimental import pallas as pl
from jax.experimental.pallas import tpu as pltpu

F32 = jnp.float32
BF16 = jnp.bfloat16

D_MODEL = 1024
DEPTH = 2
GRID_W = 64
HEAD_DIM = 64
AXIS_DIM = HEAD_DIM // 2
GQA_HEADS = 8
GQA_KV_HEADS = 2
DIFF_HEADS = 4
FOURIER_GROUPS = 4
FOURIER_GROUP_DIM = 128
FOURIER_WIDTH = FOURIER_GROUPS * FOURIER_GROUP_DIM
GQA_WIDTH = GQA_HEADS * HEAD_DIM
GQA_KV_WIDTH = GQA_KV_HEADS * HEAD_DIM
DIFF_WIDTH = DIFF_HEADS * 2 * HEAD_DIM
N_BRANCHES = 3
GATES_WIDTH = N_BRANCHES * D_MODEL
N_EXPERTS = 16
N_EXPERT_GROUPS = 4
EXPERTS_PER_GROUP = N_EXPERTS // N_EXPERT_GROUPS
D_EXPERT = 256
ROPE_THETA = 10000.0
EPS = 1e-6
QK_SCALE = HEAD_DIM ** -0.5 * math.log2(math.e)

_OFF_UF = 0
_OFF_QG = _OFF_UF + FOURIER_WIDTH
_OFF_KG = _OFF_QG + GQA_WIDTH
_OFF_VG = _OFF_KG + GQA_KV_WIDTH
_OFF_QD = _OFF_VG + GQA_KV_WIDTH
_OFF_KD = _OFF_QD + DIFF_WIDTH
_OFF_VD = _OFF_KD + DIFF_WIDTH
_OFF_GT = _OFF_VD + DIFF_WIDTH
IN_WIDTH = _OFF_GT + GATES_WIDTH

LANES = 128
SUBLANES = 8
VMEM_LIMIT_BYTES = 56 * 1024 * 1024

MOD_ROWS = 8
TOKEN_TILE = 512
MERGE_TILE = 1024
ATTN_Q_TILE = 256
ATTN_K_CHUNK = 768
ATTN_SCORE_ROWS = 768
ATTN_FILL_ROWS = 256
ATTN_MAPS_PER_GROUP = 2
ATTN_ONES_ROWS = 16
ROUTE_ROWS = 16
POS_LANE = N_EXPERTS
MOE_BLOCK = 128
DFT_RADIX = 64
DFT_S1_COLS = 8192
DFT_S2_ROWS = 16


def _params(*sem):
    return pltpu.CompilerParams(dimension_semantics=sem, vmem_limit_bytes=VMEM_LIMIT_BYTES)


def _const_spec(shape):
    nd = len(shape)
    return pl.BlockSpec(shape, lambda *_: (0,) * nd, pipeline_mode=pl.Buffered(1))


def _adaln_kernel(c_ref, w_ref, b_ref, o_ref):
    c = c_ref[...]
    s = c * jax.nn.sigmoid(c)
    o_ref[0] = jnp.dot(s, w_ref[0], preferred_element_type=F32, precision=lax.Precision.HIGHEST) + b_ref[0]


def _adaln(cond, w_mod, b_mod):
    tn = 1536
    n = 6 * D_MODEL
    return pl.pallas_call(
        _adaln_kernel,
        grid=(DEPTH, n // tn),
        in_specs=[
            pl.BlockSpec((MOD_ROWS, D_MODEL), lambda l, j: (0, 0)),
            pl.BlockSpec((1, D_MODEL, tn), lambda l, j: (l, 0, j)),
            pl.BlockSpec((1, 1, tn), lambda l, j: (l, 0, j)),
        ],
        out_specs=pl.BlockSpec((1, MOD_ROWS, tn), lambda l, j: (l, 0, j)),
        out_shape=jax.ShapeDtypeStruct((DEPTH, MOD_ROWS, n), F32),
        compiler_params=_params("arbitrary", "arbitrary"),
        name="adaln",
    )(cond, w_mod, b_mod.reshape(DEPTH, 1, n))


def _mod_spec(chunk, row0, rows_per_batch, tm):
    return pl.BlockSpec((None, None, 1, D_MODEL),
                        lambda i, *_: (row0 + (i * tm) // rows_per_batch, chunk, 0, 0))


def _rope(x, cos, sin_next, sin_prev):
    return x * cos + pltpu.roll(x, LANES - AXIS_DIM // 2, 1) * sin_next + pltpu.roll(x, AXIS_DIM // 2, 1) * sin_prev


def _proj_kernel(x_ref, g_ref, sh_ref, sc_ref, w_ref, bd_ref, qn_ref, kn_ref, cos_ref, sn_ref, sp_ref,
                 uf_ref, qg_ref, kg_ref, vg_ref, qd_ref, kd_ref, vd_ref, gt_ref, *, rope):
    x = x_ref[...]
    h = x * lax.rsqrt(jnp.mean(x * x, axis=-1, keepdims=True) + EPS) * g_ref[...]
    hb = (h * (1.0 + sc_ref[...]) + sh_ref[...]).astype(BF16)

    def mm(lo, width):
        return jnp.dot(hb, w_ref[:, lo:lo + width], preferred_element_type=F32)

    def head_norm(z, gain):
        width = z.shape[-1]
        ms = jnp.dot((z * z).astype(BF16), bd_ref[:width, :width], preferred_element_type=F32)
        return z * lax.rsqrt(ms + EPS) * gain

    def rotary(z):
        if not rope:
            return z
        cos, sn, sp = cos_ref[...], sn_ref[...], sp_ref[...]
        return jnp.concatenate(
            [_rope(z[:, c:c + LANES], cos, sn, sp) for c in range(0, z.shape[-1], LANES)], axis=-1)

    uf_ref[...] = mm(_OFF_UF, FOURIER_WIDTH).astype(uf_ref.dtype)
    qg_ref[...] = (rotary(head_norm(mm(_OFF_QG, GQA_WIDTH), qn_ref[...])) * QK_SCALE).astype(qg_ref.dtype)
    kv = mm(_OFF_KG, 2 * GQA_KV_WIDTH)
    kg_ref[...] = rotary(head_norm(kv[:, :GQA_KV_WIDTH], kn_ref[...])).astype(kg_ref.dtype)
    vg_ref[...] = kv[:, GQA_KV_WIDTH:].astype(vg_ref.dtype)
    qd_ref[...] = (rotary(mm(_OFF_QD, DIFF_WIDTH)) * QK_SCALE).astype(qd_ref.dtype)
    kd_ref[...] = rotary(mm(_OFF_KD, DIFF_WIDTH)).astype(kd_ref.dtype)
    vd_ref[...] = mm(_OFF_VD, DIFF_WIDTH).astype(vd_ref.dtype)
    for j in range(N_BRANCHES):
        z = mm(_OFF_GT + j * D_MODEL, D_MODEL)
        gt_ref[:, j * D_MODEL:(j + 1) * D_MODEL] = jax.nn.sigmoid(z).astype(gt_ref.dtype)


def _proj(x, norm_g, mod, w_in, bd, qn, kn, rope_tabs, *, row0, rows_per_batch, rope, kv_dtype):
    t = x.shape[0]
    tm = TOKEN_TILE
    cos, sn, sp = rope_tabs
    n_pos = cos.shape[0]
    row = lambda width: pl.BlockSpec((tm, width), lambda i: (i, 0))
    tab = pl.BlockSpec((tm, LANES), lambda i: (i % (n_pos // tm), 0))
    outs = [(FOURIER_WIDTH, BF16), (GQA_WIDTH, BF16), (GQA_KV_WIDTH, kv_dtype), (GQA_KV_WIDTH, kv_dtype),
            (DIFF_WIDTH, BF16), (DIFF_WIDTH, kv_dtype), (DIFF_WIDTH, kv_dtype), (GATES_WIDTH, BF16)]
    return pl.pallas_call(
        functools.partial(_proj_kernel, rope=rope),
        grid=(t // tm,),
        in_specs=[
            row(D_MODEL),
            _const_spec((1, D_MODEL)),
            _mod_spec(0, row0, rows_per_batch, tm),
            _mod_spec(1, row0, rows_per_batch, tm),
            _const_spec((D_MODEL, IN_WIDTH)),
            _const_spec((GQA_WIDTH, GQA_WIDTH)),
            _const_spec((1, GQA_WIDTH)),
            _const_spec((1, GQA_KV_WIDTH)),
            tab, tab, tab,
        ],
        out_specs=[row(w) for w, _ in outs],
        out_shape=[jax.ShapeDtypeStruct((t, w), dt) for w, dt in outs],
        compiler_params=_params("arbitrary"),
        name="proj",
    )(x, norm_g, mod, mod, w_in, bd, qn, kn, cos, sn, sp)


def _cos_sin(n_rows, n_cols, period, scale=1.0):
    r = np.arange(n_rows, dtype=np.int64)[:, None]
    c = np.arange(n_cols, dtype=np.int64)[None, :]
    ang = 2.0 * np.pi * ((r * c) % period).astype(np.float64) / period
    return np.cos(ang) * scale, np.sin(ang) * scale


def _table(a):
    return jnp.asarray(a, F32).astype(BF16)


def _dft_small_kernel(u_ref, cs_c_ref, cs_n_ref, o_ref):
    u = u_ref[0]
    n = u.shape[0]
    gd = FOURIER_GROUP_DIM
    rows = jnp.concatenate([u[:, g * gd:(g + 1) * gd] for g in range(FOURIER_GROUPS)], axis=0)
    t = jnp.dot(rows, cs_c_ref[...], preferred_element_type=F32).astype(BF16)
    stacked = jnp.concatenate(
        [jnp.concatenate([t[g * n:(g + 1) * n, :gd], t[g * n:(g + 1) * n, gd:]], axis=0)
         for g in range(FOURIER_GROUPS)], axis=1)
    o_ref[0] = jnp.dot(cs_n_ref[...], stacked, preferred_element_type=F32).astype(o_ref.dtype)


def _fourier_small(u, n_batch, n_pos):
    cc, sc = _cos_sin(FOURIER_GROUP_DIM, FOURIER_GROUP_DIM, FOURIER_GROUP_DIM,
                      scale=(n_pos * FOURIER_GROUP_DIM) ** -0.5)
    cn, sn = _cos_sin(n_pos, n_pos, n_pos)
    cs_c = _table(np.concatenate([cc, sc], axis=1))
    cs_n = _table(np.concatenate([cn, -sn], axis=1))
    blk = pl.BlockSpec((1, n_pos, FOURIER_WIDTH), lambda b: (b, 0, 0))
    out = pl.pallas_call(
        _dft_small_kernel,
        grid=(n_batch,),
        in_specs=[blk, _const_spec(cs_c.shape), _const_spec(cs_n.shape)],
        out_specs=blk,
        out_shape=jax.ShapeDtypeStruct((n_batch, n_pos, FOURIER_WIDTH), BF16),
        compiler_params=_params("arbitrary"),
        name="fourier_small",
    )(u.reshape(n_batch, n_pos, FOURIER_WIDTH), cs_c, cs_n)
    return out.reshape(n_batch * n_pos, FOURIER_WIDTH)


def _dft_stage1_kernel(u_ref, f_ref, o_ref):
    o_ref[0] = jnp.dot(f_ref[...], u_ref[0], preferred_element_type=F32).astype(o_ref.dtype)


def _dft_stage2_kernel(a_ref, m_ref, cs_ref, o_ref):
    lhs = []
    for i in range(DFT_S2_ROWS):
        a = jnp.concatenate([a_ref[0, 0, i], a_ref[0, 1, i]], axis=0)
        b = jnp.dot(m_ref[i], a, preferred_element_type=F32).astype(BF16)
        br, bi = b[:DFT_RADIX], b[DFT_RADIX:]
        for g in range(FOURIER_GROUPS):
            lo = g * FOURIER_GROUP_DIM
            lhs.append(jnp.concatenate([br[:, lo:lo + FOURIER_GROUP_DIM], bi[:, lo:lo + FOURIER_GROUP_DIM]], axis=1))
    y = jnp.dot(jnp.concatenate(lhs, axis=0), cs_ref[...], preferred_element_type=F32).astype(o_ref.dtype)
    for j in range(DFT_S2_ROWS * FOURIER_GROUPS):
        o_ref[0, :, j * FOURIER_GROUP_DIM:(j + 1) * FOURIER_GROUP_DIM] = y[j * DFT_RADIX:(j + 1) * DFT_RADIX]


def _fourier_large(u, n_batch, n_pos):
    r = DFT_RADIX
    assert n_pos == r * r
    wide = r * FOURIER_WIDTH
    c1, s1 = _cos_sin(r, r, r)
    f1 = _table(np.concatenate([c1, -s1], axis=0))
    k1 = np.arange(r, dtype=np.int64)[:, None, None]
    k2 = np.arange(r, dtype=np.int64)[None, :, None]
    n2 = np.arange(r, dtype=np.int64)[None, None, :]
    ang = 2.0 * np.pi * ((n2 * (r * k2 + k1)) % n_pos).astype(np.float64) / n_pos
    mr, mi = np.cos(ang), -np.sin(ang)
    m = _table(np.concatenate([np.concatenate([mr, -mi], axis=2),
                               np.concatenate([mi, mr], axis=2)], axis=1))
    cc, sc = _cos_sin(FOURIER_GROUP_DIM, FOURIER_GROUP_DIM, FOURIER_GROUP_DIM,
                      scale=(n_pos * FOURIER_GROUP_DIM) ** -0.5)
    cs = _table(np.concatenate([cc, sc], axis=0))

    a = pl.pallas_call(
        _dft_stage1_kernel,
        grid=(n_batch, wide // DFT_S1_COLS),
        in_specs=[pl.BlockSpec((1, r, DFT_S1_COLS), lambda b, j: (b, 0, j)), _const_spec(f1.shape)],
        out_specs=pl.BlockSpec((1, 2 * r, DFT_S1_COLS), lambda b, j: (b, 0, j)),
        out_shape=jax.ShapeDtypeStruct((n_batch, 2 * r, wide), BF16),
        compiler_params=_params("arbitrary", "arbitrary"),
        name="fourier_stage1",
    )(u.reshape(n_batch, r, wide), f1)

    kb = DFT_S2_ROWS
    out = pl.pallas_call(
        _dft_stage2_kernel,
        grid=(n_batch, r // kb),
        in_specs=[
            pl.BlockSpec((1, 2, kb, r, FOURIER_WIDTH), lambda b, j: (b, 0, j, 0, 0)),
            pl.BlockSpec((kb, 2 * r, 2 * r), lambda b, j: (j, 0, 0)),
            _const_spec(cs.shape),
        ],
        out_specs=pl.BlockSpec((1, r, kb * FOURIER_WIDTH), lambda b, j: (b, 0, j)),
        out_shape=jax.ShapeDtypeStruct((n_batch, r, wide), BF16),
        compiler_params=_params("arbitrary", "arbitrary"),
        name="fourier_stage2",
    )(a.reshape(n_batch, 2, r, r, FOURIER_WIDTH), m, cs)
    return out.reshape(n_batch * n_pos, FOURIER_WIDTH)


def _attn_kernel(*refs, n_groups, n_maps, n_kv, k_width, v_width, has_cache, diff, lambda_init, n_chunks):
    it = iter(refs)
    q_ref, kn_ref, vn_ref = next(it), next(it), next(it)
    kc_ref = vc_ref = lam_ref = sg_ref = None
    if has_cache:
        kc_ref, vc_ref = next(it), next(it)
    if diff:
        lam_ref, sg_ref = next(it), next(it)
    o_ref, kall_ref, vt_ref, rhs_ref, s0_ref, s1_ref, mx0_ref, mx1_ref, p0_ref, p1_ref = (
        next(it) for _ in range(10))
    kc_rows = kall_ref.shape[1]
    tq = q_ref.shape[1]
    w = n_maps * tq
    score_rows = min(kc_rows, ATTN_SCORE_ROWS)
    kw = LANES
    vw = v_width // n_kv
    kv_of = lambda g: g // (n_groups // n_kv)

    def key_lane(g, m):
        return kv_of(g) * 2 * HEAD_DIM + m * HEAD_DIM if diff else kv_of(g) * HEAD_DIM

    @pl.when(pl.program_id(1) == 0)
    def _():
        piece = min(kc_rows, ATTN_FILL_ROWS)
        n_cached = kc_ref.shape[1] // piece if has_cache else 0
        for i in range(n_chunks * kc_rows // piece):
            src_k, src_v, i0 = (kc_ref, vc_ref, i) if i < n_cached else (kn_ref, vn_ref, i - n_cached)
            rows = slice(i0 * piece, (i0 + 1) * piece)
            c, dst = divmod(i * piece, kc_rows)
            kall_ref[c, dst:dst + piece, :] = src_k[0, rows, :].astype(BF16)
            v_t = src_v[0, rows, :].astype(F32).T.astype(BF16)
            for h in range(n_kv):
                vt_ref[c, h, :vw, dst:dst + piece] = v_t[h * vw:(h + 1) * vw]
        for c in range(n_chunks):
            for h in range(n_kv):
                vt_ref[c, h, vw:, :] = jnp.ones((vt_ref.shape[2] - vw, kc_rows), BF16)

    qt = q_ref[0].astype(F32).T.astype(BF16)
    zeros = jnp.zeros((HEAD_DIM, tq), BF16)
    for g in range(n_groups):
        cols = []
        for m in range(n_maps):
            qrow = (g * n_maps + m) * HEAD_DIM
            parts = [zeros] * (kw // HEAD_DIM)
            parts[key_lane(g, m) % kw // HEAD_DIM] = qt[qrow:qrow + HEAD_DIM]
            cols.append(jnp.concatenate(parts, axis=0))
        rhs_ref[g] = jnp.concatenate(cols, axis=1)

    s_refs = (s0_ref, s1_ref)
    mx_refs = (mx0_ref, mx1_ref)
    p_refs = (p0_ref, p1_ref)

    def scores(c, slot):
        for g in range(n_groups):
            k_lo = key_lane(g, 0) // kw * kw
            mx = None
            for r in range(0, kc_rows, score_rows):
                s = jnp.dot(kall_ref[c, r:r + score_rows, k_lo:k_lo + kw], rhs_ref[g], preferred_element_type=F32)
                s_refs[slot][g, r:r + score_rows] = s
                blk_max = jnp.max(s, axis=0, keepdims=True)
                mx = blk_max if mx is None else jnp.maximum(mx, blk_max)
            mx_refs[slot][g] = mx

    def softmax(slot, carry):
        new = []
        for g in range(n_groups):
            m_run, _, acc = carry[g]
            m_new = jnp.maximum(m_run, mx_refs[slot][g])
            p_refs[slot][g] = jnp.exp2(s_refs[slot][g] - m_new).astype(BF16)
            new.append((m_new, jnp.exp2(m_run - m_new), acc))
        return tuple(new)

    def values(c, slot, carry):
        return tuple((m_run, alpha,
                      alpha * acc + jnp.dot(vt_ref[c, kv_of(g)], p_refs[slot][g], preferred_element_type=F32))
                     for g, (m_run, alpha, acc) in enumerate(carry))

    def tick(t, parity, carry):
        static = isinstance(t, int)
        if not static or t < n_chunks:
            scores(t, parity)
        if not static or 2 <= t < n_chunks + 2:
            carry = values(t - 2, parity, carry)
        if not static or 1 <= t < n_chunks + 1:
            carry = softmax(1 - parity, carry)
        return carry

    carry = tuple((jnp.full((1, w), -1e30, F32), jnp.ones((1, w), F32), jnp.zeros((vt_ref.shape[2], w), F32))
                  for _ in range(n_groups))
    n_steady = max(n_chunks - 2, 0)
    for t in range(2):
        carry = tick(t, t % 2, carry)
    carry = lax.fori_loop(0, n_steady // 2, lambda j, cr: tick(2 * j + 3, 1, tick(2 * j + 2, 0, cr)), carry)
    for t in range(2 + 2 * (n_steady // 2), n_chunks + 2):
        carry = tick(t, t % 2, carry)
    fin = carry

    if diff:
        lp = lam_ref[...]
        lam = (jnp.exp(jnp.sum(lp[0:1] * lp[1:2], axis=-1, keepdims=True))
               - jnp.exp(jnp.sum(lp[2:3] * lp[3:4], axis=-1, keepdims=True)) + lambda_init)
    outs = []
    for g in range(n_groups):
        acc = fin[g][2]
        o = acc[:vw] / acc[vw:vw + 1]
        if diff:
            o = o[:, :tq] - lam * o[:, tq:]
            o = o * lax.rsqrt(jnp.mean(o * o, axis=0, keepdims=True) + EPS) * sg_ref[...] * (1.0 - lambda_init)
            outs.append(o)
        else:
            outs.extend(o[:, m * tq:(m + 1) * tq] for m in range(n_maps))
    o_ref[0] = jnp.concatenate(outs, axis=0).T.astype(o_ref.dtype)


def _attention(q, k_new, v_new, cache, layer, extra, *, n_batch, n_pos, diff, lambda_init):
    k_width = k_new.shape[-1]
    v_width = v_new.shape[-1]
    tq = min(ATTN_Q_TILE, n_pos)
    kc_rows = min(ATTN_K_CHUNK, n_pos)
    has_cache = cache is not None
    past = cache[0].shape[2] if has_cache else 0
    n_keys = n_pos + past
    assert past % min(kc_rows, ATTN_FILL_ROWS) == 0 and n_keys % kc_rows == 0
    n_chunks = n_keys // kc_rows
    n_kv = DIFF_HEADS if diff else GQA_KV_HEADS
    n_maps = ATTN_MAPS_PER_GROUP
    n_groups = q.shape[-1] // (n_maps * HEAD_DIM)
    q_w = q.shape[-1]

    full = lambda width: pl.BlockSpec((1, n_pos, width), lambda b, i: (b, 0, 0))
    in_specs = [pl.BlockSpec((1, tq, q_w), lambda b, i: (b, i, 0)), full(k_width), full(v_width)]
    args = [q.reshape(n_batch, n_pos, q_w), k_new.reshape(n_batch, n_pos, k_width),
            v_new.reshape(n_batch, n_pos, v_width)]
    if has_cache:
        in_specs += [pl.BlockSpec((1, None, past, k_width), lambda b, i: (b, layer, 0, 0)),
                     pl.BlockSpec((1, None, past, v_width), lambda b, i: (b, layer, 0, 0))]
        args += list(cache)
    if diff:
        in_specs += [_const_spec(extra[0].shape), _const_spec(extra[1].shape)]
        args += list(extra)
    out = pl.pallas_call(
        functools.partial(_attn_kernel, n_groups=n_groups, n_maps=n_maps, n_kv=n_kv, k_width=k_width, v_width=v_width,
                          has_cache=has_cache, diff=diff, lambda_init=lambda_init, n_chunks=n_chunks),
        grid=(n_batch, n_pos // tq),
        in_specs=in_specs,
        out_specs=pl.BlockSpec((1, tq, q_w), lambda b, i: (b, i, 0)),
        out_shape=jax.ShapeDtypeStruct((n_batch, n_pos, q_w), BF16),
        scratch_shapes=[pltpu.VMEM((n_chunks, kc_rows, k_width), BF16),
                        pltpu.VMEM((n_chunks, n_kv, v_width // n_kv + (ATTN_ONES_ROWS if diff else 64), kc_rows), BF16),
                        pltpu.VMEM((n_groups, LANES, n_maps * tq), BF16),
                        pltpu.VMEM((n_groups, kc_rows, n_maps * tq), F32),
                        pltpu.VMEM((n_groups, kc_rows, n_maps * tq), F32),
                        pltpu.VMEM((n_groups, 1, n_maps * tq), F32),
                        pltpu.VMEM((n_groups, 1, n_maps * tq), F32),
                        pltpu.VMEM((n_groups, kc_rows, n_maps * tq), BF16),
                        pltpu.VMEM((n_groups, kc_rows, n_maps * tq), BF16)],
        compiler_params=_params("arbitrary", "arbitrary"),
        name="diff_attention" if diff else "gqa_attention",
    )(*args)
    return out.reshape(n_batch * n_pos, q_w)


def _merge_kernel(f_ref, og_ref, od_ref, gt_ref, x_ref, wf_ref, wg_ref, wd_ref, wo_ref,
                  g1_ref, n2_ref, sh2_ref, sc2_ref, xo_ref, h2_ref):
    def branch(j, a_ref, w_ref):
        gate = gt_ref[:, j * D_MODEL:(j + 1) * D_MODEL].astype(F32)
        return gate * jnp.dot(a_ref[...], w_ref[...], preferred_element_type=F32)

    merged = branch(0, f_ref, wf_ref) + branch(1, og_ref, wg_ref) + branch(2, od_ref, wd_ref)
    x = x_ref[...] + g1_ref[...] * jnp.dot(merged.astype(BF16), wo_ref[...], preferred_element_type=F32)
    xo_ref[...] = x
    h = x * lax.rsqrt(jnp.mean(x * x, axis=-1, keepdims=True) + EPS) * n2_ref[...]
    h2_ref[...] = (h * (1.0 + sc2_ref[...]) + sh2_ref[...]).astype(h2_ref.dtype)


def _merge(f, og, od, gates, x, w_f, w_go, w_do, w_out, mod, norm2_g, *, row0, rows_per_batch):
    t = x.shape[0]
    tm = MERGE_TILE
    row = lambda width: pl.BlockSpec((tm, width), lambda i: (i, 0))
    return pl.pallas_call(
        _merge_kernel,
        grid=(t // tm,),
        in_specs=[
            row(FOURIER_WIDTH), row(GQA_WIDTH), row(DIFF_WIDTH), row(GATES_WIDTH), row(D_MODEL),
            _const_spec(w_f.shape), _const_spec(w_go.shape), _const_spec(w_do.shape), _const_spec(w_out.shape),
            _mod_spec(2, row0, rows_per_batch, tm),
            _const_spec((1, D_MODEL)),
            _mod_spec(3, row0, rows_per_batch, tm),
            _mod_spec(4, row0, rows_per_batch, tm),
        ],
        out_specs=[row(D_MODEL), row(D_MODEL)],
        out_shape=[jax.ShapeDtypeStruct((t, D_MODEL), F32), jax.ShapeDtypeStruct((t, D_MODEL), BF16)],
        compiler_params=_params("arbitrary"),
        name="merge",
    )(f, og, od, gates, x, w_f, w_go, w_do, w_out, mod, norm2_g, mod, mod)


def _routing_weights(scores, sel):
    rows = [sel[e:e + 1] for e in range(N_EXPERTS)]
    group_score = []
    for g in range(N_EXPERT_GROUPS):
        a, b, c, d = rows[EXPERTS_PER_GROUP * g:EXPERTS_PER_GROUP * (g + 1)]
        hi1, lo1, hi2, lo2 = jnp.maximum(a, b), jnp.minimum(a, b), jnp.maximum(c, d), jnp.minimum(c, d)
        group_score.append(jnp.maximum(hi1, hi2) + jnp.maximum(jnp.minimum(hi1, hi2), jnp.maximum(lo1, lo2)))
    best, best_idx = group_score[0], jnp.zeros_like(group_score[0], dtype=jnp.int32)
    for g in range(1, N_EXPERT_GROUPS):
        better = group_score[g] > best
        best = jnp.where(better, group_score[g], best)
        best_idx = jnp.where(better, g, best_idx)
    picked = []
    for e in range(N_EXPERTS):
        g = e // EXPERTS_PER_GROUP
        rank = jnp.zeros_like(best_idx)
        for j in range(EXPERTS_PER_GROUP * g, EXPERTS_PER_GROUP * (g + 1)):
            if j != e:
                ahead = (rows[j] > rows[e]) | ((rows[j] == rows[e]) & (j < e))
                rank = rank + ahead.astype(jnp.int32)
        picked.append((best_idx == g) & (rank < 2))
    weight = [jnp.where(picked[e], scores[e:e + 1], 0.0) for e in range(N_EXPERTS)]
    total = weight[0]
    for e in range(1, N_EXPERTS):
        total = total + weight[e]
    row_id = lax.broadcasted_iota(jnp.int32, scores.shape, 0)
    comb = jnp.zeros_like(scores)
    for e in range(N_EXPERTS):
        comb = jnp.where(row_id == e, weight[e] / total, comb)
    return comb, best_idx


def _route_kernel(h_ref, wr_ref, rb_ref, tri_ref, comb_ref, pos_ref, seg_ref):
    tm = h_ref.shape[0]
    logits = lax.dot_general(wr_ref[...], h_ref[...], (((1,), (1,)), ((), ())),
                             preferred_element_type=F32)
    scores = jax.nn.sigmoid(logits)
    comb_t, best_idx = _routing_weights(scores, scores + rb_ref[...])

    grp = lax.broadcasted_iota(jnp.int32, (ROUTE_ROWS, tm), 0)
    member = grp == best_idx
    prefix = jnp.dot(jnp.where(member, 1.0, 0.0).astype(BF16), tri_ref[...], preferred_element_type=F32)
    counts = prefix[:, tm - 1:tm]
    row = lax.broadcasted_iota(jnp.int32, (ROUTE_ROWS, 1), 0)
    first = jnp.zeros((ROUTE_ROWS, 1), F32)
    for g in range(1, N_EXPERT_GROUPS + 1):
        first = jnp.where(row == g, jnp.sum(jnp.where(row < g, counts, 0.0), axis=0, keepdims=True), first)
    pos = jnp.sum(jnp.where(member, first + prefix - 1.0, 0.0), axis=0, keepdims=True)

    pos_ref[0] = pos.astype(jnp.int32)
    seg_ref[0] = jnp.broadcast_to(first[:SUBLANES], (SUBLANES, LANES)).astype(jnp.int32)
    stacked = jnp.concatenate([comb_t, pos, jnp.zeros((LANES - N_EXPERTS - 1, tm), F32)], axis=0)
    comb_ref[...] = stacked.T


def _route(h2, w_router_t, router_bias):
    t = h2.shape[0]
    tm = TOKEN_TILE
    n_tiles = t // tm
    tri = jnp.asarray(np.triu(np.ones((tm, tm), np.float32)), BF16)
    comb, pos, seg = pl.pallas_call(
        _route_kernel,
        grid=(n_tiles,),
        in_specs=[pl.BlockSpec((tm, D_MODEL), lambda i: (i, 0)), _const_spec(w_router_t.shape),
                  _const_spec(router_bias.shape), _const_spec(tri.shape)],
        out_specs=[pl.BlockSpec((tm, LANES), lambda i: (i, 0)),
                   pl.BlockSpec((1, 1, tm), lambda i: (i, 0, 0)),
                   pl.BlockSpec((1, SUBLANES, LANES), lambda i: (i, 0, 0))],
        out_shape=[jax.ShapeDtypeStruct((t, LANES), F32),
                   jax.ShapeDtypeStruct((n_tiles, 1, tm), jnp.int32),
                   jax.ShapeDtypeStruct((n_tiles, SUBLANES, LANES), jnp.int32)],
        compiler_params=_params("arbitrary"),
        name="route",
    )(h2, w_router_t, router_bias, tri)
    return comb, pos, seg[:, :, 0].reshape(n_tiles * SUBLANES)


def _moe_kernel(seg_ref, h_ref, x_ref, comb_ref, pos_ref, wgu_ref, wd_ref, g2_ref, fg_ref, o_ref,
                hs_ref, cs_ref, ys_ref, *, final):
    i = pl.program_id(0)
    tm = h_ref.shape[0]
    comb = comb_ref[...]
    pos_col = comb[:, POS_LANE:POS_LANE + 1].astype(jnp.int32)
    to_sorted = jnp.where(lax.broadcasted_iota(jnp.int32, (tm, tm), 0) == pos_ref[0], 1.0, 0.0).astype(BF16)
    to_token = jnp.where(lax.broadcasted_iota(jnp.int32, (tm, tm), 1) == pos_col, 1.0, 0.0).astype(BF16)

    hs_ref[...] = jnp.dot(to_sorted, h_ref[...], preferred_element_type=F32).astype(BF16)
    hi = comb.astype(BF16)
    lo = (comb - hi.astype(F32)).astype(BF16)
    cs_ref[...] = (jnp.dot(to_sorted, hi, preferred_element_type=F32)
                   + jnp.dot(to_sorted, lo, preferred_element_type=F32))
    ys_ref[...] = jnp.zeros_like(ys_ref)

    for g in range(N_EXPERT_GROUPS):
        first = seg_ref[i * SUBLANES + g]
        last = seg_ref[i * SUBLANES + g + 1]
        shift = MOE_BLOCK.bit_length() - 1
        b0 = lax.shift_right_logical(first, shift)
        n_blocks = lax.shift_right_logical(last + (MOE_BLOCK - 1), shift) - b0
        n_pairs = lax.shift_right_logical(n_blocks, 1)

        def run(b, n_rows, g=g):
            rows = pl.ds(pl.multiple_of(b * MOE_BLOCK, MOE_BLOCK), n_rows)
            cb = cs_ref[rows, :]
            hb = hs_ref[rows, :]
            acts = []
            for e in range(EXPERTS_PER_GROUP * g, EXPERTS_PER_GROUP * (g + 1)):
                a = jnp.dot(hb, wgu_ref[e], preferred_element_type=F32)
                gate, up = a[:, :D_EXPERT], a[:, D_EXPERT:]
                acts.append((gate * jax.nn.sigmoid(gate) * up * cb[:, e:e + 1]).astype(BF16))
            ys_ref[rows, :] += jnp.dot(jnp.concatenate(acts, axis=1), wd_ref[g], preferred_element_type=F32)

        def pair(j, carry, run=run, b0=b0):
            run(b0 + 2 * j, 2 * MOE_BLOCK)
            return carry

        lax.fori_loop(0, n_pairs, pair, 0)
        pl.when(n_blocks - 2 * n_pairs == 1)(functools.partial(run, b0 + 2 * n_pairs, MOE_BLOCK))

    y = x_ref[...] + g2_ref[...] * jnp.dot(to_token, ys_ref[...].astype(BF16), preferred_element_type=F32)
    if final:
        y = y * lax.rsqrt(jnp.mean(y * y, axis=-1, keepdims=True) + EPS) * fg_ref[...]
    o_ref[...] = y


def _moe(h2, x, w_router_t, router_bias, w_gu, w_d, mod, final_g, *, row0, rows_per_batch, final):
    t = x.shape[0]
    tm = TOKEN_TILE
    comb, pos, seg = _route(h2, w_router_t, router_bias)
    row = lambda width: pl.BlockSpec((tm, width), lambda i, *_: (i, 0))
    return pl.pallas_call(
        functools.partial(_moe_kernel, final=final),
        grid_spec=pltpu.PrefetchScalarGridSpec(
            num_scalar_prefetch=1,
            grid=(t // tm,),
            in_specs=[
                row(D_MODEL), row(D_MODEL), row(LANES),
                pl.BlockSpec((1, 1, tm), lambda i, *_: (i, 0, 0)),
                _const_spec(w_gu.shape), _const_spec(w_d.shape),
                _mod_spec(5, row0, rows_per_batch, tm),
                _const_spec((1, D_MODEL)),
            ],
            out_specs=row(D_MODEL),
            scratch_shapes=[pltpu.VMEM((tm, D_MODEL), BF16), pltpu.VMEM((tm, LANES), F32),
                            pltpu.VMEM((tm, D_MODEL), F32)],
        ),
        out_shape=jax.ShapeDtypeStruct((t, D_MODEL), F32),
        compiler_params=_params("arbitrary"),
        name="moe",
    )(seg, h2, x, comb, pos, w_gu, w_d, mod, final_g)


def _rope_tables(n_tokens):
    rows = n_tokens // GRID_W
    row = np.repeat(np.arange(rows), GRID_W).astype(np.float64)
    col = np.tile(np.arange(GRID_W), rows).astype(np.float64)
    inv = ROPE_THETA ** (-(np.arange(AXIS_DIM // 2, dtype=np.float64) * 2.0 / AXIS_DIM))
    ang_r = row[:, None] * inv[None, :]
    ang_c = col[:, None] * inv[None, :]
    ang = np.concatenate([ang_r, ang_r, ang_c, ang_c], axis=-1)
    ang = np.concatenate([ang, ang], axis=-1)
    first_half = (np.arange(LANES) % AXIS_DIM) < AXIS_DIM // 2
    sin = np.sin(ang)
    sin_next = np.where(first_half[None, :], -sin, 0.0)
    sin_prev = np.where(first_half[None, :], 0.0, sin)
    return tuple(jnp.asarray(a, F32) for a in (np.cos(ang), sin_next, sin_prev))


def kernel(x_prompt, x_sample, cache_gqa_k, cache_gqa_v, cache_diff_k, cache_diff_v, c, c_ctx, norm1_g, w_mod,
           b_mod, w_in, gqa_q_norm, gqa_k_norm, diff_lambda, diff_subln_g, w_fourier, w_gqa_o, w_diff_o, w_out,
           norm2_g, w_router, router_bias, w_e_gate, w_e_up, w_e_down, final_norm_g):
    n_ctx_b, n_ctx, _ = x_prompt.shape
    n_lat_b, n_lat, _ = x_sample.shape
    past = cache_gqa_k.shape[2]
    assert n_lat_b + 1 <= MOD_ROWS

    w_in_b = w_in.astype(BF16)
    w_f_b, w_go_b, w_do_b, w_out_b = (w.astype(BF16) for w in (w_fourier, w_gqa_o, w_diff_o, w_out))
    w_gu_b = jnp.concatenate([w_e_gate, w_e_up], axis=-1).astype(BF16)
    w_d_b = w_e_down.astype(BF16).reshape(DEPTH, N_EXPERT_GROUPS, EXPERTS_PER_GROUP * D_EXPERT, D_MODEL)
    w_router_t = w_router.T.astype(BF16)
    rbias = router_bias.reshape(N_EXPERTS, 1)
    head_id = np.arange(GQA_WIDTH) // HEAD_DIM
    bd = jnp.asarray((head_id[:, None] == head_id[None, :]) / HEAD_DIM, BF16)
    final_g = final_norm_g.reshape(1, D_MODEL)

    cond = jnp.concatenate([c_ctx[None, :], c, jnp.zeros((MOD_ROWS - 1 - n_lat_b, D_MODEL), F32)], axis=0)
    mod = _adaln(cond, w_mod, b_mod).reshape(DEPTH, MOD_ROWS, 6, 1, D_MODEL)

    lat_rope = _rope_tables(n_lat)
    no_rope = tuple(jnp.zeros((TOKEN_TILE, LANES), F32) for _ in range(3))

    def run_pass(x, n_batch, n_pos, row0, rope_tabs, rope, caches, kv_dtype):
        t = n_batch * n_pos
        rows_per_batch = n_pos if row0 else t
        x = x.reshape(t, D_MODEL)
        kv_out = []
        cg = cd = None
        if caches is not None:
            cg = tuple(a.reshape(n_batch, DEPTH, past, GQA_KV_WIDTH) for a in caches[:2])
            cd = tuple(a.reshape(n_batch, DEPTH, past, DIFF_WIDTH) for a in caches[2:])
        for l in range(DEPTH):
            lambda_init = 0.8 - 0.6 * math.exp(-0.3 * l)
            uf, qg, kg, vg, qd, kd, vd, gates = _proj(
                x, norm1_g[l].reshape(1, D_MODEL), mod[l], w_in_b[l], bd,
                jnp.tile(gqa_q_norm[l], GQA_HEADS).reshape(1, GQA_WIDTH),
                jnp.tile(gqa_k_norm[l], GQA_KV_HEADS).reshape(1, GQA_KV_WIDTH),
                rope_tabs, row0=row0, rows_per_batch=rows_per_batch, rope=rope, kv_dtype=kv_dtype)
            kv_out.append((kg, vg, kd, vd))
            four = _fourier_large(uf, n_batch, n_pos) if n_pos == DFT_RADIX ** 2 else _fourier_small(uf, n_batch, n_pos)
            og = _attention(qg, kg, vg, cg, l, None, n_batch=n_batch, n_pos=n_pos, diff=False, lambda_init=0.0)
            od = _attention(qd, kd, vd, cd, l,
                            (diff_lambda[l], diff_subln_g[l].reshape(2 * HEAD_DIM, 1)),
                            n_batch=n_batch, n_pos=n_pos, diff=True, lambda_init=lambda_init)
            x, h2 = _merge(four, og, od, gates, x, w_f_b[l], w_go_b[l], w_do_b[l], w_out_b[l], mod[l],
                           norm2_g[l].reshape(1, D_MODEL), row0=row0, rows_per_batch=rows_per_batch)
            x = _moe(h2, x, w_router_t, rbias, w_gu_b[l], w_d_b[l], mod[l], final_g,
                     row0=row0, rows_per_batch=rows_per_batch, final=(l == DEPTH - 1))
        return x.reshape(n_batch, n_pos, D_MODEL), kv_out

    y_prompt, kv = run_pass(x_prompt, n_ctx_b, n_ctx, 0, no_rope, False, None, F32)
    y_sample, _ = run_pass(x_sample, n_lat_b, n_lat, 1, lat_rope, True,
                           (cache_gqa_k, cache_gqa_v, cache_diff_k, cache_diff_v), BF16)

    def stack(idx, shape):
        return jnp.stack([kv[l][idx].reshape((n_ctx_b, n_ctx) + shape) for l in range(DEPTH)], axis=1)

    return (y_prompt, y_sample,
            stack(0, (GQA_KV_HEADS, HEAD_DIM)), stack(1, (GQA_KV_HEADS, HEAD_DIM)),
            stack(2, (DIFF_HEADS, 2, HEAD_DIM)), stack(3, (DIFF_HEADS, 2 * HEAD_DIM)))
```

```python
import functools
import math

import jax
import jax.numpy as jnp
import numpy as np
from jax import lax
from jax.experimental import pallas as pl
from jax.experimental.pallas import tpu as pltpu

F32 = jnp.float32
BF16 = jnp.bfloat16

D_MODEL = 1024
DEPTH = 2
GRID_W = 64
HEAD_DIM = 64
AXIS_DIM = HEAD_DIM // 2
GQA_HEADS = 8
GQA_KV_HEADS = 2
DIFF_HEADS = 4
FOURIER_GROUPS = 4
FOURIER_GROUP_DIM = 128
FOURIER_WIDTH = FOURIER_GROUPS * FOURIER_GROUP_DIM
GQA_WIDTH = GQA_HEADS * HEAD_DIM
GQA_KV_WIDTH = GQA_KV_HEADS * HEAD_DIM
DIFF_WIDTH = DIFF_HEADS * 2 * HEAD_DIM
N_BRANCHES = 3
GATES_WIDTH = N_BRANCHES * D_MODEL
N_EXPERTS = 16
N_EXPERT_GROUPS = 4
EXPERTS_PER_GROUP = N_EXPERTS // N_EXPERT_GROUPS
D_EXPERT = 256
ROPE_THETA = 10000.0
EPS = 1e-6
QK_SCALE = HEAD_DIM ** -0.5 * math.log2(math.e)

_OFF_UF = 0
_OFF_QG = _OFF_UF + FOURIER_WIDTH
_OFF_KG = _OFF_QG + GQA_WIDTH
_OFF_VG = _OFF_KG + GQA_KV_WIDTH
_OFF_QD = _OFF_VG + GQA_KV_WIDTH
_OFF_KD = _OFF_QD + DIFF_WIDTH
_OFF_VD = _OFF_KD + DIFF_WIDTH
_OFF_GT = _OFF_VD + DIFF_WIDTH
IN_WIDTH = _OFF_GT + GATES_WIDTH

LANES = 128
SUBLANES = 8
VMEM_LIMIT_BYTES = 56 * 1024 * 1024

MOD_ROWS = 8
TOKEN_TILE = 512
MERGE_TILE = 1024
ATTN_Q_TILE = 256
ATTN_K_CHUNK = 768
ATTN_SCORE_ROWS = 768
ATTN_FILL_ROWS = 256
ATTN_MAPS_PER_GROUP = 2
ATTN_ONES_ROWS = 16
ROUTE_ROWS = 16
POS_LANE = N_EXPERTS
MOE_BLOCK = 128
DFT_RADIX = 64
DFT_S1_COLS = 16384
DFT_S2_ROWS = 32


def _params(*sem):
    return pltpu.CompilerParams(dimension_semantics=sem, vmem_limit_bytes=VMEM_LIMIT_BYTES)


def _const_spec(shape):
    nd = len(shape)
    return pl.BlockSpec(shape, lambda *_: (0,) * nd, pipeline_mode=pl.Buffered(1))


def _adaln_kernel(c_ref, w_ref, b_ref, o_ref):
    c = c_ref[...]
    s = c * jax.nn.sigmoid(c)
    o_ref[0] = jnp.dot(s, w_ref[0], preferred_element_type=F32, precision=lax.Precision.HIGHEST) + b_ref[0]


def _adaln(cond, w_mod, b_mod):
    tn = 1536
    n = 6 * D_MODEL
    return pl.pallas_call(
        _adaln_kernel,
        grid=(DEPTH, n // tn),
        in_specs=[
            pl.BlockSpec((MOD_ROWS, D_MODEL), lambda l, j: (0, 0)),
            pl.BlockSpec((1, D_MODEL, tn), lambda l, j: (l, 0, j)),
            pl.BlockSpec((1, 1, tn), lambda l, j: (l, 0, j)),
        ],
        out_specs=pl.BlockSpec((1, MOD_ROWS, tn), lambda l, j: (l, 0, j)),
        out_shape=jax.ShapeDtypeStruct((DEPTH, MOD_ROWS, n), F32),
        compiler_params=_params("arbitrary", "arbitrary"),
        name="adaln",
    )(cond, w_mod, b_mod.reshape(DEPTH, 1, n))


def _mod_spec(chunk, row0, rows_per_batch, tm):
    return pl.BlockSpec((None, None, 1, D_MODEL),
                        lambda i, *_: (row0 + (i * tm) // rows_per_batch, chunk, 0, 0))


def _rope(x, cos, sin_next, sin_prev):
    return x * cos + pltpu.roll(x, LANES - AXIS_DIM // 2, 1) * sin_next + pltpu.roll(x, AXIS_DIM // 2, 1) * sin_prev


def _proj_kernel(x_ref, g_ref, sh_ref, sc_ref, w_ref, bd_ref, qn_ref, kn_ref, cos_ref, sn_ref, sp_ref,
                 uf_ref, qg_ref, kg_ref, vg_ref, qd_ref, kd_ref, vd_ref, gt_ref, *, rope):
    x = x_ref[...]
    h = x * lax.rsqrt(jnp.mean(x * x, axis=-1, keepdims=True) + EPS) * g_ref[...]
    hb = (h * (1.0 + sc_ref[...]) + sh_ref[...]).astype(BF16)

    def mm(lo, width):
        return jnp.dot(hb, w_ref[:, lo:lo + width], preferred_element_type=F32)

    def head_norm(z, gain):
        width = z.shape[-1]
        ms = jnp.dot((z * z).astype(BF16), bd_ref[:width, :width], preferred_element_type=F32)
        return z * lax.rsqrt(ms + EPS) * gain

    def rotary(z):
        if not rope:
            return z
        cos, sn, sp = cos_ref[...], sn_ref[...], sp_ref[...]
        return jnp.concatenate(
            [_rope(z[:, c:c + LANES], cos, sn, sp) for c in range(0, z.shape[-1], LANES)], axis=-1)

    uf_ref[...] = mm(_OFF_UF, FOURIER_WIDTH).astype(uf_ref.dtype)
    qg_ref[...] = (rotary(head_norm(mm(_OFF_QG, GQA_WIDTH), qn_ref[...])) * QK_SCALE).astype(qg_ref.dtype)
    kv = mm(_OFF_KG, 2 * GQA_KV_WIDTH)
    kg_ref[...] = rotary(head_norm(kv[:, :GQA_KV_WIDTH], kn_ref[...])).astype(kg_ref.dtype)
    vg_ref[...] = kv[:, GQA_KV_WIDTH:].astype(vg_ref.dtype)
    qd_ref[...] = (rotary(mm(_OFF_QD, DIFF_WIDTH)) * QK_SCALE).astype(qd_ref.dtype)
    kd_ref[...] = rotary(mm(_OFF_KD, DIFF_WIDTH)).astype(kd_ref.dtype)
    vd_ref[...] = mm(_OFF_VD, DIFF_WIDTH).astype(vd_ref.dtype)
    for j in range(N_BRANCHES):
        z = mm(_OFF_GT + j * D_MODEL, D_MODEL)
        gt_ref[:, j * D_MODEL:(j + 1) * D_MODEL] = jax.nn.sigmoid(z).astype(gt_ref.dtype)


def _proj(x, norm_g, mod, w_in, bd, qn, kn, rope_tabs, *, row0, rows_per_batch, rope, kv_dtype):
    t = x.shape[0]
    tm = TOKEN_TILE
    cos, sn, sp = rope_tabs
    n_pos = cos.shape[0]
    row = lambda width: pl.BlockSpec((tm, width), lambda i: (i, 0))
    tab = pl.BlockSpec((tm, LANES), lambda i: (i % (n_pos // tm), 0))
    outs = [(FOURIER_WIDTH, BF16), (GQA_WIDTH, BF16), (GQA_KV_WIDTH, kv_dtype), (GQA_KV_WIDTH, kv_dtype),
            (DIFF_WIDTH, BF16), (DIFF_WIDTH, kv_dtype), (DIFF_WIDTH, kv_dtype), (GATES_WIDTH, BF16)]
    return pl.pallas_call(
        functools.partial(_proj_kernel, rope=rope),
        grid=(t // tm,),
        in_specs=[
            row(D_MODEL),
            _const_spec((1, D_MODEL)),
            _mod_spec(0, row0, rows_per_batch, tm),
            _mod_spec(1, row0, rows_per_batch, tm),
            _const_spec((D_MODEL, IN_WIDTH)),
            _const_spec((GQA_WIDTH, GQA_WIDTH)),
            _const_spec((1, GQA_WIDTH)),
            _const_spec((1, GQA_KV_WIDTH)),
            tab, tab, tab,
        ],
        out_specs=[row(w) for w, _ in outs],
        out_shape=[jax.ShapeDtypeStruct((t, w), dt) for w, dt in outs],
        compiler_params=_params("arbitrary"),
        name="proj",
    )(x, norm_g, mod, mod, w_in, bd, qn, kn, cos, sn, sp)


def _cos_sin(n_rows, n_cols, period, scale=1.0):
    r = np.arange(n_rows, dtype=np.int64)[:, None]
    c = np.arange(n_cols, dtype=np.int64)[None, :]
    ang = 2.0 * np.pi * ((r * c) % period).astype(np.float64) / period
    return np.cos(ang) * scale, np.sin(ang) * scale


def _table(a):
    return jnp.asarray(a, F32).astype(BF16)


def _dft_small_kernel(u_ref, cs_c_ref, cs_n_ref, o_ref):
    u = u_ref[0]
    n = u.shape[0]
    gd = FOURIER_GROUP_DIM
    rows = jnp.concatenate([u[:, g * gd:(g + 1) * gd] for g in range(FOURIER_GROUPS)], axis=0)
    t = jnp.dot(rows, cs_c_ref[...], preferred_element_type=F32).astype(BF16)
    stacked = jnp.concatenate(
        [jnp.concatenate([t[g * n:(g + 1) * n, :gd], t[g * n:(g + 1) * n, gd:]], axis=0)
         for g in range(FOURIER_GROUPS)], axis=1)
    o_ref[0] = jnp.dot(cs_n_ref[...], stacked, preferred_element_type=F32).astype(o_ref.dtype)


def _fourier_small(u, n_batch, n_pos):
    cc, sc = _cos_sin(FOURIER_GROUP_DIM, FOURIER_GROUP_DIM, FOURIER_GROUP_DIM,
                      scale=(n_pos * FOURIER_GROUP_DIM) ** -0.5)
    cn, sn = _cos_sin(n_pos, n_pos, n_pos)
    cs_c = _table(np.concatenate([cc, sc], axis=1))
    cs_n = _table(np.concatenate([cn, -sn], axis=1))
    blk = pl.BlockSpec((1, n_pos, FOURIER_WIDTH), lambda b: (b, 0, 0))
    out = pl.pallas_call(
        _dft_small_kernel,
        grid=(n_batch,),
        in_specs=[blk, _const_spec(cs_c.shape), _const_spec(cs_n.shape)],
        out_specs=blk,
        out_shape=jax.ShapeDtypeStruct((n_batch, n_pos, FOURIER_WIDTH), BF16),
        compiler_params=_params("arbitrary"),
        name="fourier_small",
    )(u.reshape(n_batch, n_pos, FOURIER_WIDTH), cs_c, cs_n)
    return out.reshape(n_batch * n_pos, FOURIER_WIDTH)


def _dft_stage1_kernel(u_ref, f_ref, o_ref):
    o_ref[0] = jnp.dot(f_ref[...], u_ref[0], preferred_element_type=F32).astype(o_ref.dtype)


def _dft_stage2_kernel(a_ref, m_ref, cs_ref, o_ref):
    lhs = []
    for i in range(DFT_S2_ROWS):
        a = jnp.concatenate([a_ref[0, 0, i], a_ref[0, 1, i]], axis=0)
        b = jnp.dot(m_ref[i], a, preferred_element_type=F32).astype(BF16)
        br, bi = b[:DFT_RADIX], b[DFT_RADIX:]
        for g in range(FOURIER_GROUPS):
            lo = g * FOURIER_GROUP_DIM
            lhs.append(jnp.concatenate([br[:, lo:lo + FOURIER_GROUP_DIM], bi[:, lo:lo + FOURIER_GROUP_DIM]], axis=1))
    y = jnp.dot(jnp.concatenate(lhs, axis=0), cs_ref[...], preferred_element_type=F32).astype(o_ref.dtype)
    for j in range(DFT_S2_ROWS * FOURIER_GROUPS):
        o_ref[0, :, j * FOURIER_GROUP_DIM:(j + 1) * FOURIER_GROUP_DIM] = y[j * DFT_RADIX:(j + 1) * DFT_RADIX]


def _fourier_large(u, n_batch, n_pos):
    r = DFT_RADIX
    assert n_pos == r * r
    wide = r * FOURIER_WIDTH
    c1, s1 = _cos_sin(r, r, r)
    f1 = _table(np.concatenate([c1, -s1], axis=0))
    k1 = np.arange(r, dtype=np.int64)[:, None, None]
    k2 = np.arange(r, dtype=np.int64)[None, :, None]
    n2 = np.arange(r, dtype=np.int64)[None, None, :]
    ang = 2.0 * np.pi * ((n2 * (r * k2 + k1)) % n_pos).astype(np.float64) / n_pos
    mr, mi = np.cos(ang), -np.sin(ang)
    m = _table(np.concatenate([np.concatenate([mr, -mi], axis=2),
                               np.concatenate([mi, mr], axis=2)], axis=1))
    cc, sc = _cos_sin(FOURIER_GROUP_DIM, FOURIER_GROUP_DIM, FOURIER_GROUP_DIM,
                      scale=(n_pos * FOURIER_GROUP_DIM) ** -0.5)
    cs = _table(np.concatenate([cc, sc], axis=0))

    a = pl.pallas_call(
        _dft_stage1_kernel,
        grid=(n_batch, wide // DFT_S1_COLS),
        in_specs=[pl.BlockSpec((1, r, DFT_S1_COLS), lambda b, j: (b, 0, j)), _const_spec(f1.shape)],
        out_specs=pl.BlockSpec((1, 2 * r, DFT_S1_COLS), lambda b, j: (b, 0, j)),
        out_shape=jax.ShapeDtypeStruct((n_batch, 2 * r, wide), BF16),
        compiler_params=_params("arbitrary", "arbitrary"),
        name="fourier_stage1",
    )(u.reshape(n_batch, r, wide), f1)

    kb = DFT_S2_ROWS
    out = pl.pallas_call(
        _dft_stage2_kernel,
        grid=(n_batch, r // kb),
        in_specs=[
            pl.BlockSpec((1, 2, kb, r, FOURIER_WIDTH), lambda b, j: (b, 0, j, 0, 0)),
            pl.BlockSpec((kb, 2 * r, 2 * r), lambda b, j: (j, 0, 0)),
            _const_spec(cs.shape),
        ],
        out_specs=pl.BlockSpec((1, r, kb * FOURIER_WIDTH), lambda b, j: (b, 0, j)),
        out_shape=jax.ShapeDtypeStruct((n_batch, r, wide), BF16),
        compiler_params=_params("arbitrary", "arbitrary"),
        name="fourier_stage2",
    )(a.reshape(n_batch, 2, r, r, FOURIER_WIDTH), m, cs)
    return out.reshape(n_batch * n_pos, FOURIER_WIDTH)


def _attn_kernel(*refs, n_groups, n_maps, n_kv, k_width, v_width, has_cache, diff, lambda_init, n_chunks):
    it = iter(refs)
    q_ref, kn_ref, vn_ref = next(it), next(it), next(it)
    kc_ref = vc_ref = lam_ref = sg_ref = None
    if has_cache:
        kc_ref, vc_ref = next(it), next(it)
    if diff:
        lam_ref, sg_ref = next(it), next(it)
    o_ref, kall_ref, vt_ref, rhs_ref, s0_ref, s1_ref, mx0_ref, mx1_ref, p0_ref, p1_ref = (
        next(it) for _ in range(10))
    kc_rows = kall_ref.shape[1]
    tq = q_ref.shape[1]
    w = n_maps * tq
    score_rows = min(kc_rows, ATTN_SCORE_ROWS)
    kw = LANES
    vw = v_width // n_kv
    kv_of = lambda g: g // (n_groups // n_kv)

    def key_lane(g, m):
        return kv_of(g) * 2 * HEAD_DIM + m * HEAD_DIM if diff else kv_of(g) * HEAD_DIM

    @pl.when(pl.program_id(1) == 0)
    def _():
        piece = min(kc_rows, ATTN_FILL_ROWS)
        n_cached = kc_ref.shape[1] // piece if has_cache else 0
        for i in range(n_chunks * kc_rows // piece):
            src_k, src_v, i0 = (kc_ref, vc_ref, i) if i < n_cached else (kn_ref, vn_ref, i - n_cached)
            rows = slice(i0 * piece, (i0 + 1) * piece)
            c, dst = divmod(i * piece, kc_rows)
            kall_ref[c, dst:dst + piece, :] = src_k[0, rows, :].astype(BF16)
            v_t = src_v[0, rows, :].astype(F32).T.astype(BF16)
            for h in range(n_kv):
                vt_ref[c, h, :vw, dst:dst + piece] = v_t[h * vw:(h + 1) * vw]
        for c in range(n_chunks):
            for h in range(n_kv):
                vt_ref[c, h, vw:, :] = jnp.ones((vt_ref.shape[2] - vw, kc_rows), BF16)

    qt = q_ref[0].astype(F32).T.astype(BF16)
    zeros = jnp.zeros((HEAD_DIM, tq), BF16)
    for g in range(n_groups):
        cols = []
        for m in range(n_maps):
            qrow = (g * n_maps + m) * HEAD_DIM
            parts = [zeros] * (kw // HEAD_DIM)
            parts[key_lane(g, m) % kw // HEAD_DIM] = qt[qrow:qrow + HEAD_DIM]
            cols.append(jnp.concatenate(parts, axis=0))
        rhs_ref[g] = jnp.concatenate(cols, axis=1)

    s_refs = (s0_ref, s1_ref)
    mx_refs = (mx0_ref, mx1_ref)
    p_refs = (p0_ref, p1_ref)

    def scores(c, slot):
        for g in range(n_groups):
            k_lo = key_lane(g, 0) // kw * kw
            mx = None
            for r in range(0, kc_rows, score_rows):
                s = jnp.dot(kall_ref[c, r:r + score_rows, k_lo:k_lo + kw], rhs_ref[g], preferred_element_type=F32)
                s_refs[slot][g, r:r + score_rows] = s
                blk_max = jnp.max(s, axis=0, keepdims=True)
                mx = blk_max if mx is None else jnp.maximum(mx, blk_max)
            mx_refs[slot][g] = mx

    def softmax(slot, carry):
        new = []
        for g in range(n_groups):
            m_run, _, acc = carry[g]
            m_new = jnp.maximum(m_run, mx_refs[slot][g])
            p_refs[slot][g] = jnp.exp2(s_refs[slot][g] - m_new).astype(BF16)
            new.append((m_new, jnp.exp2(m_run - m_new), acc))
        return tuple(new)

    def values(c, slot, carry):
        return tuple((m_run, alpha,
                      alpha * acc + jnp.dot(vt_ref[c, kv_of(g)], p_refs[slot][g], preferred_element_type=F32))
                     for g, (m_run, alpha, acc) in enumerate(carry))

    def tick(t, parity, carry):
        static = isinstance(t, int)
        if not static or t < n_chunks:
            scores(t, parity)
        if not static or 2 <= t < n_chunks + 2:
            carry = values(t - 2, parity, carry)
        if not static or 1 <= t < n_chunks + 1:
            carry = softmax(1 - parity, carry)
        return carry

    carry = tuple((jnp.full((1, w), -1e30, F32), jnp.ones((1, w), F32), jnp.zeros((vt_ref.shape[2], w), F32))
                  for _ in range(n_groups))
    n_steady = max(n_chunks - 2, 0)
    for t in range(2):
        carry = tick(t, t % 2, carry)
    carry = lax.fori_loop(0, n_steady // 2, lambda j, cr: tick(2 * j + 3, 1, tick(2 * j + 2, 0, cr)), carry)
    for t in range(2 + 2 * (n_steady // 2), n_chunks + 2):
        carry = tick(t, t % 2, carry)
    fin = carry

    if diff:
        lp = lam_ref[...]
        lam = (jnp.exp(jnp.sum(lp[0:1] * lp[1:2], axis=-1, keepdims=True))
               - jnp.exp(jnp.sum(lp[2:3] * lp[3:4], axis=-1, keepdims=True)) + lambda_init)
    outs = []
    for g in range(n_groups):
        acc = fin[g][2]
        o = acc[:vw] / acc[vw:vw + 1]
        if diff:
            o = o[:, :tq] - lam * o[:, tq:]
            o = o * lax.rsqrt(jnp.mean(o * o, axis=0, keepdims=True) + EPS) * sg_ref[...] * (1.0 - lambda_init)
            outs.append(o)
        else:
            outs.extend(o[:, m * tq:(m + 1) * tq] for m in range(n_maps))
    o_ref[0] = jnp.concatenate(outs, axis=0).T.astype(o_ref.dtype)


def _attention(q, k_new, v_new, cache, layer, extra, *, n_batch, n_pos, diff, lambda_init):
    k_width = k_new.shape[-1]
    v_width = v_new.shape[-1]
    tq = min(ATTN_Q_TILE if diff else 2 * ATTN_Q_TILE, n_pos)
    kc_rows = min(ATTN_K_CHUNK, n_pos)
    has_cache = cache is not None
    past = cache[0].shape[2] if has_cache else 0
    n_keys = n_pos + past
    assert past % min(kc_rows, ATTN_FILL_ROWS) == 0 and n_keys % kc_rows == 0
    n_chunks = n_keys // kc_rows
    n_kv = DIFF_HEADS if diff else GQA_KV_HEADS
    n_maps = ATTN_MAPS_PER_GROUP
    n_groups = q.shape[-1] // (n_maps * HEAD_DIM)
    q_w = q.shape[-1]

    full = lambda width: pl.BlockSpec((1, n_pos, width), lambda b, i: (b, 0, 0))
    in_specs = [pl.BlockSpec((1, tq, q_w), lambda b, i: (b, i, 0)), full(k_width), full(v_width)]
    args = [q.reshape(n_batch, n_pos, q_w), k_new.reshape(n_batch, n_pos, k_width),
            v_new.reshape(n_batch, n_pos, v_width)]
    if has_cache:
        in_specs += [pl.BlockSpec((1, None, past, k_width), lambda b, i: (b, layer, 0, 0)),
                     pl.BlockSpec((1, None, past, v_width), lambda b, i: (b, layer, 0, 0))]
        args += list(cache)
    if diff:
        in_specs += [_const_spec(extra[0].shape), _const_spec(extra[1].shape)]
        args += list(extra)
    out = pl.pallas_call(
        functools.partial(_attn_kernel, n_groups=n_groups, n_maps=n_maps, n_kv=n_kv, k_width=k_width, v_width=v_width,
                          has_cache=has_cache, diff=diff, lambda_init=lambda_init, n_chunks=n_chunks),
        grid=(n_batch, n_pos // tq),
        in_specs=in_specs,
        out_specs=pl.BlockSpec((1, tq, q_w), lambda b, i: (b, i, 0)),
        out_shape=jax.ShapeDtypeStruct((n_batch, n_pos, q_w), BF16),
        scratch_shapes=[pltpu.VMEM((n_chunks, kc_rows, k_width), BF16),
                        pltpu.VMEM((n_chunks, n_kv, v_width // n_kv + (ATTN_ONES_ROWS if diff else 64), kc_rows), BF16),
                        pltpu.VMEM((n_groups, LANES, n_maps * tq), BF16),
                        pltpu.VMEM((n_groups, kc_rows, n_maps * tq), F32),
                        pltpu.VMEM((n_groups, kc_rows, n_maps * tq), F32),
                        pltpu.VMEM((n_groups, 1, n_maps * tq), F32),
                        pltpu.VMEM((n_groups, 1, n_maps * tq), F32),
                        pltpu.VMEM((n_groups, kc_rows, n_maps * tq), BF16),
                        pltpu.VMEM((n_groups, kc_rows, n_maps * tq), BF16)],
        compiler_params=_params("arbitrary", "arbitrary"),
        name="diff_attention" if diff else "gqa_attention",
    )(*args)
    return out.reshape(n_batch * n_pos, q_w)


def _merge_kernel(f_ref, og_ref, od_ref, gt_ref, x_ref, wf_ref, wg_ref, wd_ref, wo_ref,
                  g1_ref, n2_ref, sh2_ref, sc2_ref, xo_ref, h2_ref):
    def branch(j, a_ref, w_ref):
        gate = gt_ref[:, j * D_MODEL:(j + 1) * D_MODEL].astype(F32)
        return gate * jnp.dot(a_ref[...], w_ref[...], preferred_element_type=F32)

    merged = branch(0, f_ref, wf_ref) + branch(1, og_ref, wg_ref) + branch(2, od_ref, wd_ref)
    x = x_ref[...] + g1_ref[...] * jnp.dot(merged.astype(BF16), wo_ref[...], preferred_element_type=F32)
    xo_ref[...] = x
    h = x * lax.rsqrt(jnp.mean(x * x, axis=-1, keepdims=True) + EPS) * n2_ref[...]
    h2_ref[...] = (h * (1.0 + sc2_ref[...]) + sh2_ref[...]).astype(h2_ref.dtype)


def _merge(f, og, od, gates, x, w_f, w_go, w_do, w_out, mod, norm2_g, *, row0, rows_per_batch):
    t = x.shape[0]
    tm = min(MERGE_TILE, t // 8)
    row = lambda width: pl.BlockSpec((tm, width), lambda i: (i, 0))
    return pl.pallas_call(
        _merge_kernel,
        grid=(t // tm,),
        in_specs=[
            row(FOURIER_WIDTH), row(GQA_WIDTH), row(DIFF_WIDTH), row(GATES_WIDTH), row(D_MODEL),
            _const_spec(w_f.shape), _const_spec(w_go.shape), _const_spec(w_do.shape), _const_spec(w_out.shape),
            _mod_spec(2, row0, rows_per_batch, tm),
            _const_spec((1, D_MODEL)),
            _mod_spec(3, row0, rows_per_batch, tm),
            _mod_spec(4, row0, rows_per_batch, tm),
        ],
        out_specs=[row(D_MODEL), row(D_MODEL)],
        out_shape=[jax.ShapeDtypeStruct((t, D_MODEL), F32), jax.ShapeDtypeStruct((t, D_MODEL), BF16)],
        compiler_params=_params("arbitrary"),
        name="merge",
    )(f, og, od, gates, x, w_f, w_go, w_do, w_out, mod, norm2_g, mod, mod)


def _routing_weights(scores, sel):
    rows = [sel[e:e + 1] for e in range(N_EXPERTS)]
    group_score = []
    for g in range(N_EXPERT_GROUPS):
        a, b, c, d = rows[EXPERTS_PER_GROUP * g:EXPERTS_PER_GROUP * (g + 1)]
        hi1, lo1, hi2, lo2 = jnp.maximum(a, b), jnp.minimum(a, b), jnp.maximum(c, d), jnp.minimum(c, d)
        group_score.append(jnp.maximum(hi1, hi2) + jnp.maximum(jnp.minimum(hi1, hi2), jnp.maximum(lo1, lo2)))
    best, best_idx = group_score[0], jnp.zeros_like(group_score[0], dtype=jnp.int32)
    for g in range(1, N_EXPERT_GROUPS):
        better = group_score[g] > best
        best = jnp.where(better, group_score[g], best)
        best_idx = jnp.where(better, g, best_idx)
    picked = []
    for e in range(N_EXPERTS):
        g = e // EXPERTS_PER_GROUP
        rank = jnp.zeros_like(best_idx)
        for j in range(EXPERTS_PER_GROUP * g, EXPERTS_PER_GROUP * (g + 1)):
            if j != e:
                ahead = (rows[j] > rows[e]) | ((rows[j] == rows[e]) & (j < e))
                rank = rank + ahead.astype(jnp.int32)
        picked.append((best_idx == g) & (rank < 2))
    weight = [jnp.where(picked[e], scores[e:e + 1], 0.0) for e in range(N_EXPERTS)]
    total = weight[0]
    for e in range(1, N_EXPERTS):
        total = total + weight[e]
    row_id = lax.broadcasted_iota(jnp.int32, scores.shape, 0)
    comb = jnp.zeros_like(scores)
    for e in range(N_EXPERTS):
        comb = jnp.where(row_id == e, weight[e] / total, comb)
    return comb, best_idx


def _route_kernel(h_ref, wr_ref, rb_ref, tri_ref, comb_ref, pos_ref, seg_ref):
    tm = h_ref.shape[0]
    logits = lax.dot_general(wr_ref[...], h_ref[...], (((1,), (1,)), ((), ())),
                             preferred_element_type=F32)
    scores = jax.nn.sigmoid(logits)
    comb_t, best_idx = _routing_weights(scores, scores + rb_ref[...])

    grp = lax.broadcasted_iota(jnp.int32, (ROUTE_ROWS, tm), 0)
    member = grp == best_idx
    prefix = jnp.dot(jnp.where(member, 1.0, 0.0).astype(BF16), tri_ref[...], preferred_element_type=F32)
    counts = prefix[:, tm - 1:tm]
    row = lax.broadcasted_iota(jnp.int32, (ROUTE_ROWS, 1), 0)
    first = jnp.zeros((ROUTE_ROWS, 1), F32)
    for g in range(1, N_EXPERT_GROUPS + 1):
        first = jnp.where(row == g, jnp.sum(jnp.where(row < g, counts, 0.0), axis=0, keepdims=True), first)
    pos = jnp.sum(jnp.where(member, first + prefix - 1.0, 0.0), axis=0, keepdims=True)

    pos_ref[0] = pos.astype(jnp.int32)
    seg_ref[0] = jnp.broadcast_to(first[:SUBLANES], (SUBLANES, LANES)).astype(jnp.int32)
    stacked = jnp.concatenate([comb_t, pos, jnp.zeros((LANES - N_EXPERTS - 1, tm), F32)], axis=0)
    comb_ref[...] = stacked.T


def _route(h2, w_router_t, router_bias):
    t = h2.shape[0]
    tm = TOKEN_TILE
    n_tiles = t // tm
    tri = jnp.asarray(np.triu(np.ones((tm, tm), np.float32)), BF16)
    comb, pos, seg = pl.pallas_call(
        _route_kernel,
        grid=(n_tiles,),
        in_specs=[pl.BlockSpec((tm, D_MODEL), lambda i: (i, 0)), _const_spec(w_router_t.shape),
                  _const_spec(router_bias.shape), _const_spec(tri.shape)],
        out_specs=[pl.BlockSpec((tm, LANES), lambda i: (i, 0)),
                   pl.BlockSpec((1, 1, tm), lambda i: (i, 0, 0)),
                   pl.BlockSpec((1, SUBLANES, LANES), lambda i: (i, 0, 0))],
        out_shape=[jax.ShapeDtypeStruct((t, LANES), F32),
                   jax.ShapeDtypeStruct((n_tiles, 1, tm), jnp.int32),
                   jax.ShapeDtypeStruct((n_tiles, SUBLANES, LANES), jnp.int32)],
        compiler_params=_params("arbitrary"),
        name="route",
    )(h2, w_router_t, router_bias, tri)
    return comb, pos, seg[:, :, 0].reshape(n_tiles * SUBLANES)


def _moe_kernel(seg_ref, h_ref, x_ref, comb_ref, pos_ref, wgu_ref, wd_ref, g2_ref, fg_ref, o_ref,
                hs_ref, cs_ref, ys_ref, *, final):
    i = pl.program_id(0)
    tm = h_ref.shape[0]
    comb = comb_ref[...]
    pos_col = comb[:, POS_LANE:POS_LANE + 1].astype(jnp.int32)
    to_sorted = jnp.where(lax.broadcasted_iota(jnp.int32, (tm, tm), 0) == pos_ref[0], 1.0, 0.0).astype(BF16)
    to_token = jnp.where(lax.broadcasted_iota(jnp.int32, (tm, tm), 1) == pos_col, 1.0, 0.0).astype(BF16)

    hs_ref[...] = jnp.dot(to_sorted, h_ref[...], preferred_element_type=F32).astype(BF16)
    hi = comb.astype(BF16)
    lo = (comb - hi.astype(F32)).astype(BF16)
    both = jnp.dot(to_sorted, jnp.concatenate([hi, lo], axis=1), preferred_element_type=F32)
    cs_ref[...] = both[:, :LANES] + both[:, LANES:]
    ys_ref[...] = jnp.zeros_like(ys_ref)

    for g in range(N_EXPERT_GROUPS):
        first = seg_ref[i * SUBLANES + g]
        last = seg_ref[i * SUBLANES + g + 1]
        shift = MOE_BLOCK.bit_length() - 1
        b0 = lax.shift_right_logical(first, shift)
        n_blocks = lax.shift_right_logical(last + (MOE_BLOCK - 1), shift) - b0
        n_pairs = lax.shift_right_logical(n_blocks, 1)

        def run(b, n_rows, g=g):
            rows = pl.ds(pl.multiple_of(b * MOE_BLOCK, MOE_BLOCK), n_rows)
            cb = cs_ref[rows, :]
            hb = hs_ref[rows, :]
            acts = []
            for e in range(EXPERTS_PER_GROUP * g, EXPERTS_PER_GROUP * (g + 1)):
                a = jnp.dot(hb, wgu_ref[e], preferred_element_type=F32)
                gate, up = a[:, :D_EXPERT], a[:, D_EXPERT:]
                acts.append((gate * jax.nn.sigmoid(gate) * up * cb[:, e:e + 1]).astype(BF16))
            ys_ref[rows, :] += jnp.dot(jnp.concatenate(acts, axis=1), wd_ref[g], preferred_element_type=F32)

        def pair(j, carry, run=run, b0=b0):
            run(b0 + 2 * j, 2 * MOE_BLOCK)
            return carry

        lax.fori_loop(0, n_pairs, pair, 0)
        pl.when(n_blocks - 2 * n_pairs == 1)(functools.partial(run, b0 + 2 * n_pairs, MOE_BLOCK))

    y = x_ref[...] + g2_ref[...] * jnp.dot(to_token, ys_ref[...].astype(BF16), preferred_element_type=F32)
    if final:
        y = y * lax.rsqrt(jnp.mean(y * y, axis=-1, keepdims=True) + EPS) * fg_ref[...]
    o_ref[...] = y


def _moe(h2, x, w_router_t, router_bias, w_gu, w_d, mod, final_g, *, row0, rows_per_batch, final):
    t = x.shape[0]
    tm = TOKEN_TILE
    comb, pos, seg = _route(h2, w_router_t, router_bias)
    row = lambda width: pl.BlockSpec((tm, width), lambda i, *_: (i, 0))
    return pl.pallas_call(
        functools.partial(_moe_kernel, final=final),
        grid_spec=pltpu.PrefetchScalarGridSpec(
            num_scalar_prefetch=1,
            grid=(t // tm,),
            in_specs=[
                row(D_MODEL), row(D_MODEL), row(LANES),
                pl.BlockSpec((1, 1, tm), lambda i, *_: (i, 0, 0)),
                _const_spec(w_gu.shape), _const_spec(w_d.shape),
                _mod_spec(5, row0, rows_per_batch, tm),
                _const_spec((1, D_MODEL)),
            ],
            out_specs=row(D_MODEL),
            scratch_shapes=[pltpu.VMEM((tm, D_MODEL), BF16), pltpu.VMEM((tm, LANES), F32),
                            pltpu.VMEM((tm, D_MODEL), F32)],
        ),
        out_shape=jax.ShapeDtypeStruct((t, D_MODEL), F32),
        compiler_params=_params("arbitrary"),
        name="moe",
    )(seg, h2, x, comb, pos, w_gu, w_d, mod, final_g)


def _rope_tables(n_tokens):
    rows = n_tokens // GRID_W
    row = np.repeat(np.arange(rows), GRID_W).astype(np.float64)
    col = np.tile(np.arange(GRID_W), rows).astype(np.float64)
    inv = ROPE_THETA ** (-(np.arange(AXIS_DIM // 2, dtype=np.float64) * 2.0 / AXIS_DIM))
    ang_r = row[:, None] * inv[None, :]
    ang_c = col[:, None] * inv[None, :]
    ang = np.concatenate([ang_r, ang_r, ang_c, ang_c], axis=-1)
    ang = np.concatenate([ang, ang], axis=-1)
    first_half = (np.arange(LANES) % AXIS_DIM) < AXIS_DIM // 2
    sin = np.sin(ang)
    sin_next = np.where(first_half[None, :], -sin, 0.0)
    sin_prev = np.where(first_half[None, :], 0.0, sin)
    return tuple(jnp.asarray(a, F32) for a in (np.cos(ang), sin_next, sin_prev))


def kernel(x_prompt, x_sample, cache_gqa_k, cache_gqa_v, cache_diff_k, cache_diff_v, c, c_ctx, norm1_g, w_mod,
           b_mod, w_in, gqa_q_norm, gqa_k_norm, diff_lambda, diff_subln_g, w_fourier, w_gqa_o, w_diff_o, w_out,
           norm2_g, w_router, router_bias, w_e_gate, w_e_up, w_e_down, final_norm_g):
    n_ctx_b, n_ctx, _ = x_prompt.shape
    n_lat_b, n_lat, _ = x_sample.shape
    past = cache_gqa_k.shape[2]
    assert n_lat_b + 1 <= MOD_ROWS

    w_in_b = w_in.astype(BF16)
    w_f_b, w_go_b, w_do_b, w_out_b = (w.astype(BF16) for w in (w_fourier, w_gqa_o, w_diff_o, w_out))
    w_gu_b = jnp.concatenate([w_e_gate, w_e_up], axis=-1).astype(BF16)
    w_d_b = w_e_down.astype(BF16).reshape(DEPTH, N_EXPERT_GROUPS, EXPERTS_PER_GROUP * D_EXPERT, D_MODEL)
    w_router_t = w_router.T.astype(BF16)
    rbias = router_bias.reshape(N_EXPERTS, 1)
    head_id = np.arange(GQA_WIDTH) // HEAD_DIM
    bd = jnp.asarray((head_id[:, None] == head_id[None, :]) / HEAD_DIM, BF16)
    final_g = final_norm_g.reshape(1, D_MODEL)

    cond = jnp.concatenate([c_ctx[None, :], c, jnp.zeros((MOD_ROWS - 1 - n_lat_b, D_MODEL), F32)], axis=0)
    mod = _adaln(cond, w_mod, b_mod).reshape(DEPTH, MOD_ROWS, 6, 1, D_MODEL)

    lat_rope = _rope_tables(n_lat)
    no_rope = tuple(jnp.zeros((TOKEN_TILE, LANES), F32) for _ in range(3))

    def run_pass(x, n_batch, n_pos, row0, rope_tabs, rope, caches, kv_dtype):
        t = n_batch * n_pos
        rows_per_batch = n_pos if row0 else t
        x = x.reshape(t, D_MODEL)
        kv_out = []
        cg = cd = None
        if caches is not None:
            cg = tuple(a.reshape(n_batch, DEPTH, past, GQA_KV_WIDTH) for a in caches[:2])
            cd = tuple(a.reshape(n_batch, DEPTH, past, DIFF_WIDTH) for a in caches[2:])
        for l in range(DEPTH):
            lambda_init = 0.8 - 0.6 * math.exp(-0.3 * l)
            uf, qg, kg, vg, qd, kd, vd, gates = _proj(
                x, norm1_g[l].reshape(1, D_MODEL), mod[l], w_in_b[l], bd,
                jnp.tile(gqa_q_norm[l], GQA_HEADS).reshape(1, GQA_WIDTH),
                jnp.tile(gqa_k_norm[l], GQA_KV_HEADS).reshape(1, GQA_KV_WIDTH),
                rope_tabs, row0=row0, rows_per_batch=rows_per_batch, rope=rope, kv_dtype=kv_dtype)
            kv_out.append((kg, vg, kd, vd))
            four = _fourier_large(uf, n_batch, n_pos) if n_pos == DFT_RADIX ** 2 else _fourier_small(uf, n_batch, n_pos)
            og = _attention(qg, kg, vg, cg, l, None, n_batch=n_batch, n_pos=n_pos, diff=False, lambda_init=0.0)
            od = _attention(qd, kd, vd, cd, l,
                            (diff_lambda[l], diff_subln_g[l].reshape(2 * HEAD_DIM, 1)),
                            n_batch=n_batch, n_pos=n_pos, diff=True, lambda_init=lambda_init)
            x, h2 = _merge(four, og, od, gates, x, w_f_b[l], w_go_b[l], w_do_b[l], w_out_b[l], mod[l],
                           norm2_g[l].reshape(1, D_MODEL), row0=row0, rows_per_batch=rows_per_batch)
            x = _moe(h2, x, w_router_t, rbias, w_gu_b[l], w_d_b[l], mod[l], final_g,
                     row0=row0, rows_per_batch=rows_per_batch, final=(l == DEPTH - 1))
        return x.reshape(n_batch, n_pos, D_MODEL), kv_out

    y_prompt, kv = run_pass(x_prompt, n_ctx_b, n_ctx, 0, no_rope, False, None, F32)
    y_sample, _ = run_pass(x_sample, n_lat_b, n_lat, 1, lat_rope, True,
                           (cache_gqa_k, cache_gqa_v, cache_diff_k, cache_diff_v), BF16)

    def stack(idx, shape):
        return jnp.stack([kv[l][idx].reshape((n_ctx_b, n_ctx) + shape) for l in range(DEPTH)], axis=1)

    return (y_prompt, y_sample,
            stack(0, (GQA_KV_HEADS, HEAD_DIM)), stack(1, (GQA_KV_HEADS, HEAD_DIM)),
            stack(2, (DIFF_HEADS, 2, HEAD_DIM)), stack(3, (DIFF_HEADS, 2 * HEAD_DIM)))
```

```python
import functools
import math

import jax
import jax.numpy as jnp
import numpy as np
from jax import lax
from jax.experimental import pallas as pl
from jax.experimental.pallas import tpu as pltpu

F32 = jnp.float32
BF16 = jnp.bfloat16

D_MODEL = 1024
DEPTH = 2
GRID_W = 64
HEAD_DIM = 64
AXIS_DIM = HEAD_DIM // 2
GQA_HEADS = 8
GQA_KV_HEADS = 2
DIFF_HEADS = 4
FOURIER_GROUPS = 4
FOURIER_GROUP_DIM = 128
FOURIER_WIDTH = FOURIER_GROUPS * FOURIER_GROUP_DIM
GQA_WIDTH = GQA_HEADS * HEAD_DIM
GQA_KV_WIDTH = GQA_KV_HEADS * HEAD_DIM
DIFF_WIDTH = DIFF_HEADS * 2 * HEAD_DIM
N_BRANCHES = 3
GATES_WIDTH = N_BRANCHES * D_MODEL
N_EXPERTS = 16
N_EXPERT_GROUPS = 4
EXPERTS_PER_GROUP = N_EXPERTS // N_EXPERT_GROUPS
D_EXPERT = 256
ROPE_THETA = 10000.0
EPS = 1e-6
QK_SCALE = HEAD_DIM ** -0.5 * math.log2(math.e)

_OFF_UF = 0
_OFF_QG = _OFF_UF + FOURIER_WIDTH
_OFF_KG = _OFF_QG + GQA_WIDTH
_OFF_VG = _OFF_KG + GQA_KV_WIDTH
_OFF_QD = _OFF_VG + GQA_KV_WIDTH
_OFF_KD = _OFF_QD + DIFF_WIDTH
_OFF_VD = _OFF_KD + DIFF_WIDTH
_OFF_GT = _OFF_VD + DIFF_WIDTH
IN_WIDTH = _OFF_GT + GATES_WIDTH

LANES = 128
SUBLANES = 8
VMEM_LIMIT_BYTES = 56 * 1024 * 1024

MOD_ROWS = 8
TOKEN_TILE = 512
MERGE_TILE = 1024
ATTN_Q_TILE = (256, 512)
ATTN_K_CHUNK = (768, 512)
ATTN_SCORE_ROWS = 768
SOFTMAX_ROWS = 32
ATTN_FILL_ROWS = 256
ATTN_MAPS_PER_GROUP = 2
ATTN_ONES_ROWS = (16, 64)
ROUTE_ROWS = 16
POS_LANE = N_EXPERTS
MOE_BLOCK = 128
DFT_RADIX = 64
DFT_S1_COLS = 16384
DFT_S2_ROWS = 32


def _params(*sem):
    return pltpu.CompilerParams(dimension_semantics=sem, vmem_limit_bytes=VMEM_LIMIT_BYTES)


def _const_spec(shape):
    nd = len(shape)
    return pl.BlockSpec(shape, lambda *_: (0,) * nd, pipeline_mode=pl.Buffered(1))


def _adaln_kernel(c_ref, w_ref, b_ref, o_ref):
    c = c_ref[...]
    s = c * jax.nn.sigmoid(c)
    o_ref[0] = jnp.dot(s, w_ref[0], preferred_element_type=F32, precision=lax.Precision.HIGHEST) + b_ref[0]


def _adaln(cond, w_mod, b_mod):
    tn = 1536
    n = 6 * D_MODEL
    return pl.pallas_call(
        _adaln_kernel,
        grid=(DEPTH, n // tn),
        in_specs=[
            pl.BlockSpec((MOD_ROWS, D_MODEL), lambda l, j: (0, 0)),
            pl.BlockSpec((1, D_MODEL, tn), lambda l, j: (l, 0, j)),
            pl.BlockSpec((1, 1, tn), lambda l, j: (l, 0, j)),
        ],
        out_specs=pl.BlockSpec((1, MOD_ROWS, tn), lambda l, j: (l, 0, j)),
        out_shape=jax.ShapeDtypeStruct((DEPTH, MOD_ROWS, n), F32),
        compiler_params=_params("arbitrary", "arbitrary"),
        name="adaln",
    )(cond, w_mod, b_mod.reshape(DEPTH, 1, n))


def _mod_spec(chunk, row0, rows_per_batch, tm):
    return pl.BlockSpec((None, None, 1, D_MODEL),
                        lambda i, *_: (row0 + (i * tm) // rows_per_batch, chunk, 0, 0))


def _rope(x, cos, sin_next, sin_prev):
    return x * cos + pltpu.roll(x, LANES - AXIS_DIM // 2, 1) * sin_next + pltpu.roll(x, AXIS_DIM // 2, 1) * sin_prev


def _proj_kernel(x_ref, g_ref, sh_ref, sc_ref, w_ref, bd_ref, qn_ref, kn_ref, cos_ref, sn_ref, sp_ref,
                 uf_ref, qg_ref, kg_ref, vg_ref, qd_ref, kd_ref, vd_ref, gt_ref, *, rope):
    x = x_ref[...]
    h = x * lax.rsqrt(jnp.mean(x * x, axis=-1, keepdims=True) + EPS) * g_ref[...]
    hb = (h * (1.0 + sc_ref[...]) + sh_ref[...]).astype(BF16)

    def mm(lo, width):
        return jnp.dot(hb, w_ref[:, lo:lo + width], preferred_element_type=F32)

    def head_norm(z, gain):
        width = z.shape[-1]
        ms = jnp.dot((z * z).astype(BF16), bd_ref[:width, :width], preferred_element_type=F32)
        return z * lax.rsqrt(ms + EPS) * gain

    def rotary(z):
        if not rope:
            return z
        cos, sn, sp = cos_ref[...], sn_ref[...], sp_ref[...]
        return jnp.concatenate(
            [_rope(z[:, c:c + LANES], cos, sn, sp) for c in range(0, z.shape[-1], LANES)], axis=-1)

    uf_ref[...] = mm(_OFF_UF, FOURIER_WIDTH).astype(uf_ref.dtype)
    qg_ref[...] = (rotary(head_norm(mm(_OFF_QG, GQA_WIDTH), qn_ref[...])) * QK_SCALE).astype(qg_ref.dtype)
    kv = mm(_OFF_KG, 2 * GQA_KV_WIDTH)
    kg_ref[...] = rotary(head_norm(kv[:, :GQA_KV_WIDTH], kn_ref[...])).astype(kg_ref.dtype)
    vg_ref[...] = kv[:, GQA_KV_WIDTH:].astype(vg_ref.dtype)
    qd_ref[...] = (rotary(mm(_OFF_QD, DIFF_WIDTH)) * QK_SCALE).astype(qd_ref.dtype)
    kd_ref[...] = rotary(mm(_OFF_KD, DIFF_WIDTH)).astype(kd_ref.dtype)
    vd_ref[...] = mm(_OFF_VD, DIFF_WIDTH).astype(vd_ref.dtype)
    for j in range(N_BRANCHES):
        z = mm(_OFF_GT + j * D_MODEL, D_MODEL)
        gt_ref[:, j * D_MODEL:(j + 1) * D_MODEL] = jax.nn.sigmoid(z).astype(gt_ref.dtype)


def _proj(x, norm_g, mod, w_in, bd, qn, kn, rope_tabs, *, row0, rows_per_batch, rope, kv_dtype):
    t = x.shape[0]
    tm = TOKEN_TILE
    cos, sn, sp = rope_tabs
    n_pos = cos.shape[0]
    row = lambda width: pl.BlockSpec((tm, width), lambda i: (i, 0))
    tab = pl.BlockSpec((tm, LANES), lambda i: (i % (n_pos // tm), 0))
    outs = [(FOURIER_WIDTH, BF16), (GQA_WIDTH, BF16), (GQA_KV_WIDTH, kv_dtype), (GQA_KV_WIDTH, kv_dtype),
            (DIFF_WIDTH, BF16), (DIFF_WIDTH, kv_dtype), (DIFF_WIDTH, kv_dtype), (GATES_WIDTH, BF16)]
    return pl.pallas_call(
        functools.partial(_proj_kernel, rope=rope),
        grid=(t // tm,),
        in_specs=[
            row(D_MODEL),
            _const_spec((1, D_MODEL)),
            _mod_spec(0, row0, rows_per_batch, tm),
            _mod_spec(1, row0, rows_per_batch, tm),
            _const_spec((D_MODEL, IN_WIDTH)),
            _const_spec((GQA_WIDTH, GQA_WIDTH)),
            _const_spec((1, GQA_WIDTH)),
            _const_spec((1, GQA_KV_WIDTH)),
            tab, tab, tab,
        ],
        out_specs=[row(w) for w, _ in outs],
        out_shape=[jax.ShapeDtypeStruct((t, w), dt) for w, dt in outs],
        compiler_params=_params("arbitrary"),
        name="proj",
    )(x, norm_g, mod, mod, w_in, bd, qn, kn, cos, sn, sp)


def _cos_sin(n_rows, n_cols, period, scale=1.0):
    r = np.arange(n_rows, dtype=np.int64)[:, None]
    c = np.arange(n_cols, dtype=np.int64)[None, :]
    ang = 2.0 * np.pi * ((r * c) % period).astype(np.float64) / period
    return np.cos(ang) * scale, np.sin(ang) * scale


def _table(a):
    return jnp.asarray(a, F32).astype(BF16)


def _dft_small_kernel(u_ref, cs_c_ref, cs_n_ref, o_ref):
    u = u_ref[0]
    n = u.shape[0]
    gd = FOURIER_GROUP_DIM
    rows = jnp.concatenate([u[:, g * gd:(g + 1) * gd] for g in range(FOURIER_GROUPS)], axis=0)
    t = jnp.dot(rows, cs_c_ref[...], preferred_element_type=F32).astype(BF16)
    stacked = jnp.concatenate(
        [jnp.concatenate([t[g * n:(g + 1) * n, :gd], t[g * n:(g + 1) * n, gd:]], axis=0)
         for g in range(FOURIER_GROUPS)], axis=1)
    o_ref[0] = jnp.dot(cs_n_ref[...], stacked, preferred_element_type=F32).astype(o_ref.dtype)


def _fourier_small(u, n_batch, n_pos):
    cc, sc = _cos_sin(FOURIER_GROUP_DIM, FOURIER_GROUP_DIM, FOURIER_GROUP_DIM,
                      scale=(n_pos * FOURIER_GROUP_DIM) ** -0.5)
    cn, sn = _cos_sin(n_pos, n_pos, n_pos)
    cs_c = _table(np.concatenate([cc, sc], axis=1))
    cs_n = _table(np.concatenate([cn, -sn], axis=1))
    blk = pl.BlockSpec((1, n_pos, FOURIER_WIDTH), lambda b: (b, 0, 0))
    out = pl.pallas_call(
        _dft_small_kernel,
        grid=(n_batch,),
        in_specs=[blk, _const_spec(cs_c.shape), _const_spec(cs_n.shape)],
        out_specs=blk,
        out_shape=jax.ShapeDtypeStruct((n_batch, n_pos, FOURIER_WIDTH), BF16),
        compiler_params=_params("arbitrary"),
        name="fourier_small",
    )(u.reshape(n_batch, n_pos, FOURIER_WIDTH), cs_c, cs_n)
    return out.reshape(n_batch * n_pos, FOURIER_WIDTH)


def _dft_stage1_kernel(u_ref, f_ref, o_ref):
    o_ref[0] = jnp.dot(f_ref[...], u_ref[0], preferred_element_type=F32).astype(o_ref.dtype)


def _dft_stage2_kernel(a_ref, m_ref, cs_ref, o_ref):
    lhs = []
    for i in range(DFT_S2_ROWS):
        a = jnp.concatenate([a_ref[0, 0, i], a_ref[0, 1, i]], axis=0)
        b = jnp.dot(m_ref[i], a, preferred_element_type=F32).astype(BF16)
        br, bi = b[:DFT_RADIX], b[DFT_RADIX:]
        for g in range(FOURIER_GROUPS):
            lo = g * FOURIER_GROUP_DIM
            lhs.append(jnp.concatenate([br[:, lo:lo + FOURIER_GROUP_DIM], bi[:, lo:lo + FOURIER_GROUP_DIM]], axis=1))
    y = jnp.dot(jnp.concatenate(lhs, axis=0), cs_ref[...], preferred_element_type=F32).astype(o_ref.dtype)
    for j in range(DFT_S2_ROWS * FOURIER_GROUPS):
        o_ref[0, :, j * FOURIER_GROUP_DIM:(j + 1) * FOURIER_GROUP_DIM] = y[j * DFT_RADIX:(j + 1) * DFT_RADIX]


def _fourier_large(u, n_batch, n_pos):
    r = DFT_RADIX
    assert n_pos == r * r
    wide = r * FOURIER_WIDTH
    c1, s1 = _cos_sin(r, r, r)
    f1 = _table(np.concatenate([c1, -s1], axis=0))
    k1 = np.arange(r, dtype=np.int64)[:, None, None]
    k2 = np.arange(r, dtype=np.int64)[None, :, None]
    n2 = np.arange(r, dtype=np.int64)[None, None, :]
    ang = 2.0 * np.pi * ((n2 * (r * k2 + k1)) % n_pos).astype(np.float64) / n_pos
    mr, mi = np.cos(ang), -np.sin(ang)
    m = _table(np.concatenate([np.concatenate([mr, -mi], axis=2),
                               np.concatenate([mi, mr], axis=2)], axis=1))
    cc, sc = _cos_sin(FOURIER_GROUP_DIM, FOURIER_GROUP_DIM, FOURIER_GROUP_DIM,
                      scale=(n_pos * FOURIER_GROUP_DIM) ** -0.5)
    cs = _table(np.concatenate([cc, sc], axis=0))

    a = pl.pallas_call(
        _dft_stage1_kernel,
        grid=(n_batch, wide // DFT_S1_COLS),
        in_specs=[pl.BlockSpec((1, r, DFT_S1_COLS), lambda b, j: (b, 0, j)), _const_spec(f1.shape)],
        out_specs=pl.BlockSpec((1, 2 * r, DFT_S1_COLS), lambda b, j: (b, 0, j)),
        out_shape=jax.ShapeDtypeStruct((n_batch, 2 * r, wide), BF16),
        compiler_params=_params("arbitrary", "arbitrary"),
        name="fourier_stage1",
    )(u.reshape(n_batch, r, wide), f1)

    kb = DFT_S2_ROWS
    out = pl.pallas_call(
        _dft_stage2_kernel,
        grid=(n_batch, r // kb),
        in_specs=[
            pl.BlockSpec((1, 2, kb, r, FOURIER_WIDTH), lambda b, j: (b, 0, j, 0, 0)),
            pl.BlockSpec((kb, 2 * r, 2 * r), lambda b, j: (j, 0, 0)),
            _const_spec(cs.shape),
        ],
        out_specs=pl.BlockSpec((1, r, kb * FOURIER_WIDTH), lambda b, j: (b, 0, j)),
        out_shape=jax.ShapeDtypeStruct((n_batch, r, wide), BF16),
        compiler_params=_params("arbitrary", "arbitrary"),
        name="fourier_stage2",
    )(a.reshape(n_batch, 2, r, r, FOURIER_WIDTH), m, cs)
    return out.reshape(n_batch * n_pos, FOURIER_WIDTH)


def _attn_kernel(*refs, n_groups, n_maps, n_kv, k_width, v_width, has_cache, diff, lambda_init, n_chunks):
    it = iter(refs)
    q_ref, kn_ref, vn_ref = next(it), next(it), next(it)
    kc_ref = vc_ref = lam_ref = sg_ref = None
    if has_cache:
        kc_ref, vc_ref = next(it), next(it)
    if diff:
        lam_ref, sg_ref = next(it), next(it)
    o_ref, kall_ref, vt_ref, rhs_ref, s0_ref, s1_ref, mx0_ref, mx1_ref, p0_ref, p1_ref = (
        next(it) for _ in range(10))
    kc_rows = kall_ref.shape[1]
    tq = q_ref.shape[1]
    w = n_maps * tq
    score_rows = min(kc_rows, ATTN_SCORE_ROWS)
    kw = LANES
    vw = v_width // n_kv
    kv_of = lambda g: g // (n_groups // n_kv)

    def key_lane(g, m):
        return kv_of(g) * 2 * HEAD_DIM + m * HEAD_DIM if diff else kv_of(g) * HEAD_DIM

    @pl.when(pl.program_id(1) == 0)
    def _():
        piece = min(kc_rows, ATTN_FILL_ROWS)
        n_cached = kc_ref.shape[1] // piece if has_cache else 0
        for i in range(n_chunks * kc_rows // piece):
            src_k, src_v, i0 = (kc_ref, vc_ref, i) if i < n_cached else (kn_ref, vn_ref, i - n_cached)
            rows = slice(i0 * piece, (i0 + 1) * piece)
            c, dst = divmod(i * piece, kc_rows)
            kall_ref[c, dst:dst + piece, :] = src_k[0, rows, :].astype(BF16)
            v_t = src_v[0, rows, :].astype(F32).T.astype(BF16)
            for h in range(n_kv):
                vt_ref[c, h, :vw, dst:dst + piece] = v_t[h * vw:(h + 1) * vw]
        for c in range(n_chunks):
            for h in range(n_kv):
                vt_ref[c, h, vw:, :] = jnp.ones((vt_ref.shape[2] - vw, kc_rows), BF16)

    qt = q_ref[0].astype(F32).T.astype(BF16)
    zeros = jnp.zeros((HEAD_DIM, tq), BF16)
    for g in range(n_groups):
        cols = []
        for m in range(n_maps):
            qrow = (g * n_maps + m) * HEAD_DIM
            parts = [zeros] * (kw // HEAD_DIM)
            parts[key_lane(g, m) % kw // HEAD_DIM] = qt[qrow:qrow + HEAD_DIM]
            cols.append(jnp.concatenate(parts, axis=0))
        rhs_ref[g] = jnp.concatenate(cols, axis=1)

    s_refs = (s0_ref, s1_ref)
    mx_refs = (mx0_ref, mx1_ref)
    p_refs = (p0_ref, p1_ref)

    def scores(c, slot):
        for g in range(n_groups):
            k_lo = key_lane(g, 0) // kw * kw
            mx = None
            for r in range(0, kc_rows, score_rows):
                s = jnp.dot(kall_ref[c, r:r + score_rows, k_lo:k_lo + kw], rhs_ref[g], preferred_element_type=F32)
                s_refs[slot][g, r:r + score_rows] = s
                blk_max = jnp.max(s, axis=0, keepdims=True)
                mx = blk_max if mx is None else jnp.maximum(mx, blk_max)
            mx_refs[slot][g] = mx

    def softmax(slot, carry):
        new = []
        for g in range(n_groups):
            m_run, _, acc = carry[g]
            m_new = jnp.maximum(m_run, mx_refs[slot][g])
            for r in range(0, kc_rows, SOFTMAX_ROWS):
                rows = slice(r, min(r + SOFTMAX_ROWS, kc_rows))
                p_refs[slot][g, rows, :] = jnp.exp2(s_refs[slot][g, rows, :] - m_new).astype(BF16)
            new.append((m_new, jnp.exp2(m_run - m_new), acc))
        return tuple(new)

    def values(c, slot, carry):
        return tuple((m_run, alpha,
                      alpha * acc + jnp.dot(vt_ref[c, kv_of(g)], p_refs[slot][g], preferred_element_type=F32))
                     for g, (m_run, alpha, acc) in enumerate(carry))

    def tick(t, parity, carry):
        static = isinstance(t, int)
        if not static or t < n_chunks:
            scores(t, parity)
        if not static or 2 <= t < n_chunks + 2:
            carry = values(t - 2, parity, carry)
        if not static or 1 <= t < n_chunks + 1:
            carry = softmax(1 - parity, carry)
        return carry

    carry = tuple((jnp.full((1, w), -1e30, F32), jnp.ones((1, w), F32), jnp.zeros((vt_ref.shape[2], w), F32))
                  for _ in range(n_groups))
    n_steady = max(n_chunks - 2, 0)
    for t in range(2):
        carry = tick(t, t % 2, carry)
    carry = lax.fori_loop(0, n_steady // 2, lambda j, cr: tick(2 * j + 3, 1, tick(2 * j + 2, 0, cr)), carry)
    for t in range(2 + 2 * (n_steady // 2), n_chunks + 2):
        carry = tick(t, t % 2, carry)
    fin = carry

    if diff:
        lp = lam_ref[...]
        lam = (jnp.exp(jnp.sum(lp[0:1] * lp[1:2], axis=-1, keepdims=True))
               - jnp.exp(jnp.sum(lp[2:3] * lp[3:4], axis=-1, keepdims=True)) + lambda_init)
    outs = []
    for g in range(n_groups):
        acc = fin[g][2]
        o = acc[:vw] / acc[vw:vw + 1]
        if diff:
            o = o[:, :tq] - lam * o[:, tq:]
            o = o * lax.rsqrt(jnp.mean(o * o, axis=0, keepdims=True) + EPS) * sg_ref[...] * (1.0 - lambda_init)
            outs.append(o)
        else:
            outs.extend(o[:, m * tq:(m + 1) * tq] for m in range(n_maps))
    o_ref[0] = jnp.concatenate(outs, axis=0).T.astype(o_ref.dtype)


def _attention(q, k_new, v_new, cache, layer, extra, *, n_batch, n_pos, diff, lambda_init):
    k_width = k_new.shape[-1]
    v_width = v_new.shape[-1]
    kind = 0 if diff else 1
    tq = min(ATTN_Q_TILE[kind], n_pos)
    kc_rows = min(ATTN_K_CHUNK[kind], n_pos)
    has_cache = cache is not None
    past = cache[0].shape[2] if has_cache else 0
    n_keys = n_pos + past
    assert past % min(kc_rows, ATTN_FILL_ROWS) == 0 and n_keys % kc_rows == 0
    n_chunks = n_keys // kc_rows
    n_kv = DIFF_HEADS if diff else GQA_KV_HEADS
    n_maps = ATTN_MAPS_PER_GROUP
    n_groups = q.shape[-1] // (n_maps * HEAD_DIM)
    q_w = q.shape[-1]

    full = lambda width: pl.BlockSpec((1, n_pos, width), lambda b, i: (b, 0, 0))
    in_specs = [pl.BlockSpec((1, tq, q_w), lambda b, i: (b, i, 0)), full(k_width), full(v_width)]
    args = [q.reshape(n_batch, n_pos, q_w), k_new.reshape(n_batch, n_pos, k_width),
            v_new.reshape(n_batch, n_pos, v_width)]
    if has_cache:
        in_specs += [pl.BlockSpec((1, None, past, k_width), lambda b, i: (b, layer, 0, 0)),
                     pl.BlockSpec((1, None, past, v_width), lambda b, i: (b, layer, 0, 0))]
        args += list(cache)
    if diff:
        in_specs += [_const_spec(extra[0].shape), _const_spec(extra[1].shape)]
        args += list(extra)
    out = pl.pallas_call(
        functools.partial(_attn_kernel, n_groups=n_groups, n_maps=n_maps, n_kv=n_kv, k_width=k_width, v_width=v_width,
                          has_cache=has_cache, diff=diff, lambda_init=lambda_init, n_chunks=n_chunks),
        grid=(n_batch, n_pos // tq),
        in_specs=in_specs,
        out_specs=pl.BlockSpec((1, tq, q_w), lambda b, i: (b, i, 0)),
        out_shape=jax.ShapeDtypeStruct((n_batch, n_pos, q_w), BF16),
        scratch_shapes=[pltpu.VMEM((n_chunks, kc_rows, k_width), BF16),
                        pltpu.VMEM((n_chunks, n_kv, v_width // n_kv + ATTN_ONES_ROWS[kind], kc_rows), BF16),
                        pltpu.VMEM((n_groups, LANES, n_maps * tq), BF16),
                        pltpu.VMEM((n_groups, kc_rows, n_maps * tq), F32),
                        pltpu.VMEM((n_groups, kc_rows, n_maps * tq), F32),
                        pltpu.VMEM((n_groups, 1, n_maps * tq), F32),
                        pltpu.VMEM((n_groups, 1, n_maps * tq), F32),
                        pltpu.VMEM((n_groups, kc_rows, n_maps * tq), BF16),
                        pltpu.VMEM((n_groups, kc_rows, n_maps * tq), BF16)],
        compiler_params=_params("arbitrary", "arbitrary"),
        name="diff_attention" if diff else "gqa_attention",
    )(*args)
    return out.reshape(n_batch * n_pos, q_w)


def _merge_kernel(f_ref, og_ref, od_ref, gt_ref, x_ref, wf_ref, wg_ref, wd_ref, wo_ref,
                  g1_ref, n2_ref, sh2_ref, sc2_ref, xo_ref, h2_ref):
    def branch(j, a_ref, w_ref):
        gate = gt_ref[:, j * D_MODEL:(j + 1) * D_MODEL].astype(F32)
        return gate * jnp.dot(a_ref[...], w_ref[...], preferred_element_type=F32)

    merged = branch(0, f_ref, wf_ref) + branch(1, og_ref, wg_ref) + branch(2, od_ref, wd_ref)
    x = x_ref[...] + g1_ref[...] * jnp.dot(merged.astype(BF16), wo_ref[...], preferred_element_type=F32)
    xo_ref[...] = x
    h = x * lax.rsqrt(jnp.mean(x * x, axis=-1, keepdims=True) + EPS) * n2_ref[...]
    h2_ref[...] = (h * (1.0 + sc2_ref[...]) + sh2_ref[...]).astype(h2_ref.dtype)


def _merge(f, og, od, gates, x, w_f, w_go, w_do, w_out, mod, norm2_g, *, row0, rows_per_batch):
    t = x.shape[0]
    tm = min(MERGE_TILE, t // 8)
    row = lambda width: pl.BlockSpec((tm, width), lambda i: (i, 0))
    return pl.pallas_call(
        _merge_kernel,
        grid=(t // tm,),
        in_specs=[
            row(FOURIER_WIDTH), row(GQA_WIDTH), row(DIFF_WIDTH), row(GATES_WIDTH), row(D_MODEL),
            _const_spec(w_f.shape), _const_spec(w_go.shape), _const_spec(w_do.shape), _const_spec(w_out.shape),
            _mod_spec(2, row0, rows_per_batch, tm),
            _const_spec((1, D_MODEL)),
            _mod_spec(3, row0, rows_per_batch, tm),
            _mod_spec(4, row0, rows_per_batch, tm),
        ],
        out_specs=[row(D_MODEL), row(D_MODEL)],
        out_shape=[jax.ShapeDtypeStruct((t, D_MODEL), F32), jax.ShapeDtypeStruct((t, D_MODEL), BF16)],
        compiler_params=_params("arbitrary"),
        name="merge",
    )(f, og, od, gates, x, w_f, w_go, w_do, w_out, mod, norm2_g, mod, mod)


def _routing_weights(scores, sel):
    rows = [sel[e:e + 1] for e in range(N_EXPERTS)]
    group_score = []
    for g in range(N_EXPERT_GROUPS):
        a, b, c, d = rows[EXPERTS_PER_GROUP * g:EXPERTS_PER_GROUP * (g + 1)]
        hi1, lo1, hi2, lo2 = jnp.maximum(a, b), jnp.minimum(a, b), jnp.maximum(c, d), jnp.minimum(c, d)
        group_score.append(jnp.maximum(hi1, hi2) + jnp.maximum(jnp.minimum(hi1, hi2), jnp.maximum(lo1, lo2)))
    best, best_idx = group_score[0], jnp.zeros_like(group_score[0], dtype=jnp.int32)
    for g in range(1, N_EXPERT_GROUPS):
        better = group_score[g] > best
        best = jnp.where(better, group_score[g], best)
        best_idx = jnp.where(better, g, best_idx)
    picked = []
    for e in range(N_EXPERTS):
        g = e // EXPERTS_PER_GROUP
        rank = jnp.zeros_like(best_idx)
        for j in range(EXPERTS_PER_GROUP * g, EXPERTS_PER_GROUP * (g + 1)):
            if j != e:
                ahead = (rows[j] > rows[e]) | ((rows[j] == rows[e]) & (j < e))
                rank = rank + ahead.astype(jnp.int32)
        picked.append((best_idx == g) & (rank < 2))
    weight = [jnp.where(picked[e], scores[e:e + 1], 0.0) for e in range(N_EXPERTS)]
    total = weight[0]
    for e in range(1, N_EXPERTS):
        total = total + weight[e]
    row_id = lax.broadcasted_iota(jnp.int32, scores.shape, 0)
    comb = jnp.zeros_like(scores)
    for e in range(N_EXPERTS):
        comb = jnp.where(row_id == e, weight[e] / total, comb)
    return comb, best_idx


def _route_kernel(h_ref, wr_ref, rb_ref, tri_ref, comb_ref, pos_ref, seg_ref):
    tm = h_ref.shape[0]
    logits = lax.dot_general(wr_ref[...], h_ref[...], (((1,), (1,)), ((), ())),
                             preferred_element_type=F32)
    scores = jax.nn.sigmoid(logits)
    comb_t, best_idx = _routing_weights(scores, scores + rb_ref[...])

    grp = lax.broadcasted_iota(jnp.int32, (ROUTE_ROWS, tm), 0)
    member = grp == best_idx
    prefix = jnp.dot(jnp.where(member, 1.0, 0.0).astype(BF16), tri_ref[...], preferred_element_type=F32)
    counts = prefix[:, tm - 1:tm]
    row = lax.broadcasted_iota(jnp.int32, (ROUTE_ROWS, 1), 0)
    first = jnp.zeros((ROUTE_ROWS, 1), F32)
    for g in range(1, N_EXPERT_GROUPS + 1):
        first = jnp.where(row == g, jnp.sum(jnp.where(row < g, counts, 0.0), axis=0, keepdims=True), first)
    pos = jnp.sum(jnp.where(member, first + prefix - 1.0, 0.0), axis=0, keepdims=True)

    pos_ref[0] = pos.astype(jnp.int32)
    seg_ref[0] = jnp.broadcast_to(first[:SUBLANES], (SUBLANES, LANES)).astype(jnp.int32)
    stacked = jnp.concatenate([comb_t, pos, jnp.zeros((LANES - N_EXPERTS - 1, tm), F32)], axis=0)
    comb_ref[...] = stacked.T


def _route(h2, w_router_t, router_bias):
    t = h2.shape[0]
    tm = TOKEN_TILE
    n_tiles = t // tm
    tri = jnp.asarray(np.triu(np.ones((tm, tm), np.float32)), BF16)
    comb, pos, seg = pl.pallas_call(
        _route_kernel,
        grid=(n_tiles,),
        in_specs=[pl.BlockSpec((tm, D_MODEL), lambda i: (i, 0)), _const_spec(w_router_t.shape),
                  _const_spec(router_bias.shape), _const_spec(tri.shape)],
        out_specs=[pl.BlockSpec((tm, LANES), lambda i: (i, 0)),
                   pl.BlockSpec((1, 1, tm), lambda i: (i, 0, 0)),
                   pl.BlockSpec((1, SUBLANES, LANES), lambda i: (i, 0, 0))],
        out_shape=[jax.ShapeDtypeStruct((t, LANES), F32),
                   jax.ShapeDtypeStruct((n_tiles, 1, tm), jnp.int32),
                   jax.ShapeDtypeStruct((n_tiles, SUBLANES, LANES), jnp.int32)],
        compiler_params=_params("arbitrary"),
        name="route",
    )(h2, w_router_t, router_bias, tri)
    return comb, pos, seg[:, :, 0].reshape(n_tiles * SUBLANES)


def _moe_kernel(seg_ref, h_ref, x_ref, comb_ref, pos_ref, wgu_ref, wd_ref, g2_ref, fg_ref, o_ref,
                hs_ref, cs_ref, ys_ref, *, final):
    i = pl.program_id(0)
    tm = h_ref.shape[0]
    comb = comb_ref[...]
    pos_col = comb[:, POS_LANE:POS_LANE + 1].astype(jnp.int32)
    to_sorted = jnp.where(lax.broadcasted_iota(jnp.int32, (tm, tm), 0) == pos_ref[0], 1.0, 0.0).astype(BF16)
    to_token = jnp.where(lax.broadcasted_iota(jnp.int32, (tm, tm), 1) == pos_col, 1.0, 0.0).astype(BF16)

    hs_ref[...] = jnp.dot(to_sorted, h_ref[...], preferred_element_type=F32).astype(BF16)
    hi = comb.astype(BF16)
    lo = (comb - hi.astype(F32)).astype(BF16)
    both = jnp.dot(to_sorted, jnp.concatenate([hi, lo], axis=1), preferred_element_type=F32)
    cs_ref[...] = both[:, :LANES] + both[:, LANES:]
    ys_ref[...] = jnp.zeros_like(ys_ref)

    for g in range(N_EXPERT_GROUPS):
        first = seg_ref[i * SUBLANES + g]
        last = seg_ref[i * SUBLANES + g + 1]
        shift = MOE_BLOCK.bit_length() - 1
        b0 = lax.shift_right_logical(first, shift)
        n_blocks = lax.shift_right_logical(last + (MOE_BLOCK - 1), shift) - b0
        n_pairs = lax.shift_right_logical(n_blocks, 1)

        def run(b, n_rows, g=g):
            rows = pl.ds(pl.multiple_of(b * MOE_BLOCK, MOE_BLOCK), n_rows)
            cb = cs_ref[rows, :]
            hb = hs_ref[rows, :]
            acts = []
            for e in range(EXPERTS_PER_GROUP * g, EXPERTS_PER_GROUP * (g + 1)):
                a = jnp.dot(hb, wgu_ref[e], preferred_element_type=F32)
                gate, up = a[:, :D_EXPERT], a[:, D_EXPERT:]
                acts.append((gate * jax.nn.sigmoid(gate) * up * cb[:, e:e + 1]).astype(BF16))
            ys_ref[rows, :] += jnp.dot(jnp.concatenate(acts, axis=1), wd_ref[g], preferred_element_type=F32)

        def pair(j, carry, run=run, b0=b0):
            run(b0 + 2 * j, 2 * MOE_BLOCK)
            return carry

        lax.fori_loop(0, n_pairs, pair, 0)
        pl.when(n_blocks - 2 * n_pairs == 1)(functools.partial(run, b0 + 2 * n_pairs, MOE_BLOCK))

    y = x_ref[...] + g2_ref[...] * jnp.dot(to_token, ys_ref[...].astype(BF16), preferred_element_type=F32)
    if final:
        y = y * lax.rsqrt(jnp.mean(y * y, axis=-1, keepdims=True) + EPS) * fg_ref[...]
    o_ref[...] = y


def _moe(h2, x, w_router_t, router_bias, w_gu, w_d, mod, final_g, *, row0, rows_per_batch, final):
    t = x.shape[0]
    tm = TOKEN_TILE
    comb, pos, seg = _route(h2, w_router_t, router_bias)
    row = lambda width: pl.BlockSpec((tm, width), lambda i, *_: (i, 0))
    return pl.pallas_call(
        functools.partial(_moe_kernel, final=final),
        grid_spec=pltpu.PrefetchScalarGridSpec(
            num_scalar_prefetch=1,
            grid=(t // tm,),
            in_specs=[
                row(D_MODEL), row(D_MODEL), row(LANES),
                pl.BlockSpec((1, 1, tm), lambda i, *_: (i, 0, 0)),
                _const_spec(w_gu.shape), _const_spec(w_d.shape),
                _mod_spec(5, row0, rows_per_batch, tm),
                _const_spec((1, D_MODEL)),
            ],
            out_specs=row(D_MODEL),
            scratch_shapes=[pltpu.VMEM((tm, D_MODEL), BF16), pltpu.VMEM((tm, LANES), F32),
                            pltpu.VMEM((tm, D_MODEL), F32)],
        ),
        out_shape=jax.ShapeDtypeStruct((t, D_MODEL), F32),
        compiler_params=_params("arbitrary"),
        name="moe",
    )(seg, h2, x, comb, pos, w_gu, w_d, mod, final_g)


def _rope_tables(n_tokens):
    rows = n_tokens // GRID_W
    row = np.repeat(np.arange(rows), GRID_W).astype(np.float64)
    col = np.tile(np.arange(GRID_W), rows).astype(np.float64)
    inv = ROPE_THETA ** (-(np.arange(AXIS_DIM // 2, dtype=np.float64) * 2.0 / AXIS_DIM))
    ang_r = row[:, None] * inv[None, :]
    ang_c = col[:, None] * inv[None, :]
    ang = np.concatenate([ang_r, ang_r, ang_c, ang_c], axis=-1)
    ang = np.concatenate([ang, ang], axis=-1)
    first_half = (np.arange(LANES) % AXIS_DIM) < AXIS_DIM // 2
    sin = np.sin(ang)
    sin_next = np.where(first_half[None, :], -sin, 0.0)
    sin_prev = np.where(first_half[None, :], 0.0, sin)
    return tuple(jnp.asarray(a, F32) for a in (np.cos(ang), sin_next, sin_prev))


def kernel(x_prompt, x_sample, cache_gqa_k, cache_gqa_v, cache_diff_k, cache_diff_v, c, c_ctx, norm1_g, w_mod,
           b_mod, w_in, gqa_q_norm, gqa_k_norm, diff_lambda, diff_subln_g, w_fourier, w_gqa_o, w_diff_o, w_out,
           norm2_g, w_router, router_bias, w_e_gate, w_e_up, w_e_down, final_norm_g):
    n_ctx_b, n_ctx, _ = x_prompt.shape
    n_lat_b, n_lat, _ = x_sample.shape
    past = cache_gqa_k.shape[2]
    assert n_lat_b + 1 <= MOD_ROWS

    w_in_b = w_in.astype(BF16)
    w_f_b, w_go_b, w_do_b, w_out_b = (w.astype(BF16) for w in (w_fourier, w_gqa_o, w_diff_o, w_out))
    w_gu_b = jnp.concatenate([w_e_gate, w_e_up], axis=-1).astype(BF16)
    w_d_b = w_e_down.astype(BF16).reshape(DEPTH, N_EXPERT_GROUPS, EXPERTS_PER_GROUP * D_EXPERT, D_MODEL)
    w_router_t = w_router.T.astype(BF16)
    rbias = router_bias.reshape(N_EXPERTS, 1)
    head_id = np.arange(GQA_WIDTH) // HEAD_DIM
    bd = jnp.asarray((head_id[:, None] == head_id[None, :]) / HEAD_DIM, BF16)
    final_g = final_norm_g.reshape(1, D_MODEL)

    cond = jnp.concatenate([c_ctx[None, :], c, jnp.zeros((MOD_ROWS - 1 - n_lat_b, D_MODEL), F32)], axis=0)
    mod = _adaln(cond, w_mod, b_mod).reshape(DEPTH, MOD_ROWS, 6, 1, D_MODEL)

    lat_rope = _rope_tables(n_lat)
    no_rope = tuple(jnp.zeros((TOKEN_TILE, LANES), F32) for _ in range(3))

    def run_pass(x, n_batch, n_pos, row0, rope_tabs, rope, caches, kv_dtype):
        t = n_batch * n_pos
        rows_per_batch = n_pos if row0 else t
        x = x.reshape(t, D_MODEL)
        kv_out = []
        cg = cd = None
        if caches is not None:
            cg = tuple(a.reshape(n_batch, DEPTH, past, GQA_KV_WIDTH) for a in caches[:2])
            cd = tuple(a.reshape(n_batch, DEPTH, past, DIFF_WIDTH) for a in caches[2:])
        for l in range(DEPTH):
            lambda_init = 0.8 - 0.6 * math.exp(-0.3 * l)
            uf, qg, kg, vg, qd, kd, vd, gates = _proj(
                x, norm1_g[l].reshape(1, D_MODEL), mod[l], w_in_b[l], bd,
                jnp.tile(gqa_q_norm[l], GQA_HEADS).reshape(1, GQA_WIDTH),
                jnp.tile(gqa_k_norm[l], GQA_KV_HEADS).reshape(1, GQA_KV_WIDTH),
                rope_tabs, row0=row0, rows_per_batch=rows_per_batch, rope=rope, kv_dtype=kv_dtype)
            kv_out.append((kg, vg, kd, vd))
            four = _fourier_large(uf, n_batch, n_pos) if n_pos == DFT_RADIX ** 2 else _fourier_small(uf, n_batch, n_pos)
            og = _attention(qg, kg, vg, cg, l, None, n_batch=n_batch, n_pos=n_pos, diff=False, lambda_init=0.0)
            od = _attention(qd, kd, vd, cd, l,
                            (diff_lambda[l], diff_subln_g[l].reshape(2 * HEAD_DIM, 1)),
                            n_batch=n_batch, n_pos=n_pos, diff=True, lambda_init=lambda_init)
            x, h2 = _merge(four, og, od, gates, x, w_f_b[l], w_go_b[l], w_do_b[l], w_out_b[l], mod[l],
                           norm2_g[l].reshape(1, D_MODEL), row0=row0, rows_per_batch=rows_per_batch)
            x = _moe(h2, x, w_router_t, rbias, w_gu_b[l], w_d_b[l], mod[l], final_g,
                     row0=row0, rows_per_batch=rows_per_batch, final=(l == DEPTH - 1))
        return x.reshape(n_batch, n_pos, D_MODEL), kv_out

    y_prompt, kv = run_pass(x_prompt, n_ctx_b, n_ctx, 0, no_rope, False, None, F32)
    y_sample, _ = run_pass(x_sample, n_lat_b, n_lat, 1, lat_rope, True,
                           (cache_gqa_k, cache_gqa_v, cache_diff_k, cache_diff_v), BF16)

    def stack(idx, shape):
        return jnp.stack([kv[l][idx].reshape((n_ctx_b, n_ctx) + shape) for l in range(DEPTH)], axis=1)

    return (y_prompt, y_sample,
            stack(0, (GQA_KV_HEADS, HEAD_DIM)), stack(1, (GQA_KV_HEADS, HEAD_DIM)),
            stack(2, (DIFF_HEADS, 2, HEAD_DIM)), stack(3, (DIFF_HEADS, 2 * HEAD_DIM)))
```
